```python
import jax, jax.numpy as jnp
from jax import lax
import numpy as np

D_MODEL = 1024
BATCH = 8
SEQ = 4096
DEPTH = 4

N_META = 16
BLOCK = 128
WINDOW = 128
ROPE_THETA = 10000.0
EPS = 1e-6
NEG = -1e30
SWA_HEADS = 8
SWA_KV_HEADS = 2
SWA_HEAD_DIM = 64
SWA_GROUP = SWA_HEADS // SWA_KV_HEADS
MLA_HEADS = 8
MLA_Q_RANK = 256
MLA_KV_RANK = 128
MLA_NOPE_DIM = 64
MLA_ROPE_DIM = 32
MLA_V_DIM = 64
MLA_QK_DIM = MLA_NOPE_DIM + MLA_ROPE_DIM
SWA_Q_W = SWA_HEADS * SWA_HEAD_DIM
SWA_KV_W = SWA_KV_HEADS * SWA_HEAD_DIM
MLA_OUT_W = MLA_HEADS * MLA_V_DIM
MIX_W = SWA_Q_W + MLA_OUT_W
IN_W = SWA_Q_W + 2 * SWA_KV_W + MLA_Q_RANK + MLA_KV_RANK + MLA_ROPE_DIM
D_FF = -(-8 * D_MODEL // (3 * 256)) * 256

kernel_name = "hymba_swa_sink_mla_hybrid"


def rmsnorm(x, g):
    xf = x.astype(jnp.float32)
    y = xf * lax.rsqrt(jnp.mean(xf * xf, axis=-1, keepdims=True) + EPS)
    return (y * g.astype(jnp.float32)).astype(x.dtype)


def rope(x, pos):
    d = x.shape[-1]
    inv = ROPE_THETA ** (-jnp.arange(0, d, 2, dtype=jnp.float32) / d)
    ang = pos[:, None] * inv[None, :]
    cos = jnp.cos(ang)[:, None, :]
    sin = jnp.sin(ang)[:, None, :]
    xf = x.astype(jnp.float32)
    x1, x2 = xf[..., : d // 2], xf[..., d // 2:]
    return jnp.concatenate([x1 * cos - x2 * sin, x2 * cos + x1 * sin], -1).astype(x.dtype)


def swa_sink_attention(q, k, v, sinks, key_valid):
    B, T, _, D = q.shape
    nb = T // BLOCK
    qb = q.reshape(B, nb, BLOCK, SWA_KV_HEADS, SWA_GROUP, D)
    kb = k.reshape(B, nb, BLOCK, SWA_KV_HEADS, D)
    vb = v.reshape(B, nb, BLOCK, SWA_KV_HEADS, D)
    prev = lambda a: jnp.concatenate([jnp.zeros_like(a[:, :1]), a[:, :-1]], axis=1)
    kw = jnp.concatenate([prev(kb), kb], axis=2)
    vw = jnp.concatenate([prev(vb), vb], axis=2)
    s = jnp.einsum("bnqhgd,bnkhd->bhgnqk", qb, kw,
                   preferred_element_type=jnp.float32) * (D ** -0.5)
    qpos = jnp.arange(T).reshape(nb, BLOCK)
    kpos = jnp.concatenate([qpos - BLOCK, qpos], axis=1)
    kv_ok = key_valid.reshape(nb, BLOCK)
    kv_ok = jnp.concatenate(
        [jnp.concatenate([jnp.zeros((1, BLOCK), bool), kv_ok[:-1]], 0), kv_ok], 1)
    diff = qpos[:, :, None] - kpos[:, None, :]
    mask = (diff >= 0) & (diff < WINDOW) & kv_ok[:, None, :]
    s = jnp.where(mask, s, NEG)
    sink = jnp.broadcast_to(
        sinks.astype(jnp.float32).reshape(SWA_KV_HEADS, SWA_GROUP)[None, :, :, None, None, None],
        s.shape[:-1] + (1,))
    p = jax.nn.softmax(jnp.concatenate([s, sink], axis=-1), axis=-1)[..., :-1]
    o = jnp.einsum("bhgnqk,bnkhd->bnqhgd", p.astype(v.dtype), vw)
    return o.reshape(B, T, SWA_HEADS * D)


def causal_block_attention(q, k, v, key_valid):
    B, T, H, dqk = q.shape
    nb = T // BLOCK
    scale = dqk ** -0.5
    qb = jnp.moveaxis(q.reshape(B, nb, BLOCK, H, dqk), 1, 0)
    kpos = jnp.arange(T)

    def one_block(args):
        qblk, i = args
        s = jnp.einsum("bqhd,bkhd->bhqk", qblk, k,
                       preferred_element_type=jnp.float32) * scale
        qpos = i * BLOCK + jnp.arange(BLOCK)
        mask = (kpos[None, :] <= qpos[:, None]) & key_valid[None, :]
        p = jax.nn.softmax(jnp.where(mask, s, NEG), axis=-1)
        return jnp.einsum("bhqk,bkhd->bqhd", p.astype(v.dtype), v)

    o = lax.map(one_block, (qb, jnp.arange(nb)))
    return jnp.moveaxis(o, 0, 1).reshape(B, T, H * v.shape[-1])


def setup_inputs(seed: int = 0) -> dict:
    key = jax.random.key(seed)
    ks = jax.random.split(key, 20)
    f32 = jnp.float32
    nrm = lambda k, shape, scale: jax.random.normal(k, shape, f32) * scale
    gain = lambda k, shape: 1.0 + 0.02 * jax.random.normal(k, shape, f32)
    return {
        "x": nrm(ks[0], (BATCH, SEQ, D_MODEL), 1.0),
        "meta_tokens": nrm(ks[1], (N_META, D_MODEL), 1.0),
        "attn_norm": gain(ks[2], (DEPTH, D_MODEL)),
        "w_in": nrm(ks[3], (DEPTH, D_MODEL, IN_W), D_MODEL ** -0.5),
        "q_norm": gain(ks[4], (DEPTH, MLA_Q_RANK)),
        "w_q_up": nrm(ks[5], (DEPTH, MLA_Q_RANK, MLA_HEADS * MLA_QK_DIM), MLA_Q_RANK ** -0.5),
        "kv_norm": gain(ks[6], (DEPTH, MLA_KV_RANK)),
        "w_kv_up": nrm(ks[7], (DEPTH, MLA_KV_RANK, MLA_HEADS * (MLA_NOPE_DIM + MLA_V_DIM)),
                        MLA_KV_RANK ** -0.5),
        "sinks": nrm(ks[8], (DEPTH, SWA_HEADS), 1.0),
        "out_norm_swa": gain(ks[9], (DEPTH, SWA_Q_W)),
        "out_norm_mla": gain(ks[10], (DEPTH, MLA_OUT_W)),
        "w_o": nrm(ks[11], (DEPTH, MIX_W, D_MODEL), MIX_W ** -0.5),
        "ffn_norm": gain(ks[12], (DEPTH, D_MODEL)),
        "w_gate": nrm(ks[13], (DEPTH, D_MODEL, D_FF), D_MODEL ** -0.5),
        "w_up": nrm(ks[14], (DEPTH, D_MODEL, D_FF), D_MODEL ** -0.5),
        "w_down": nrm(ks[15], (DEPTH, D_FF, D_MODEL), D_FF ** -0.5),
        "final_norm": gain(ks[16], (D_MODEL,)),
    }


def reference(x, meta_tokens, attn_norm, w_in, q_norm, w_q_up, kv_norm, w_kv_up, sinks,
              out_norm_swa, out_norm_mla, w_o, ffn_norm, w_gate, w_up, w_down, final_norm):
    B, S, D = x.shape
    front = (-N_META) % BLOCK
    back = (-S) % BLOCK
    T = front + N_META + S + back
    h = jnp.concatenate([
        jnp.zeros((B, front, D), x.dtype),
        jnp.broadcast_to(meta_tokens.astype(x.dtype)[None], (B, N_META, D)),
        x,
        jnp.zeros((B, back, D), x.dtype)], axis=1)
    idx = jnp.arange(T)
    key_valid = (idx >= front) & (idx < front + N_META + S)
    pos = (idx - front).astype(jnp.float32)

    o1 = SWA_Q_W
    o2 = o1 + SWA_KV_W
    o3 = o2 + SWA_KV_W
    o4 = o3 + MLA_Q_RANK
    o5 = o4 + MLA_KV_RANK
    for l in range(DEPTH):
        u = rmsnorm(h, attn_norm[l])
        proj = u @ w_in[l]
        q_a = rope(proj[..., :o1].reshape(B, T, SWA_HEADS, SWA_HEAD_DIM), pos)
        k_a = rope(proj[..., o1:o2].reshape(B, T, SWA_KV_HEADS, SWA_HEAD_DIM), pos)
        v_a = proj[..., o2:o3].reshape(B, T, SWA_KV_HEADS, SWA_HEAD_DIM)
        out_a = swa_sink_attention(q_a, k_a, v_a, sinks[l], key_valid)
        q_b = (rmsnorm(proj[..., o3:o4], q_norm[l]) @ w_q_up[l]).reshape(
            B, T, MLA_HEADS, MLA_QK_DIM)
        kv_b = (rmsnorm(proj[..., o4:o5], kv_norm[l]) @ w_kv_up[l]).reshape(
            B, T, MLA_HEADS, MLA_NOPE_DIM + MLA_V_DIM)
        k_rope = rope(proj[..., o5:][:, :, None, :], pos)
        q_full = jnp.concatenate(
            [q_b[..., :MLA_NOPE_DIM], rope(q_b[..., MLA_NOPE_DIM:], pos)], axis=-1)
        k_full = jnp.concatenate(
            [kv_b[..., :MLA_NOPE_DIM],
             jnp.broadcast_to(k_rope, (B, T, MLA_HEADS, MLA_ROPE_DIM))], axis=-1)
        v_b = kv_b[..., MLA_NOPE_DIM:]
        out_b = causal_block_attention(q_full, k_full, v_b, key_valid)
        mix = jnp.concatenate([rmsnorm(out_a, out_norm_swa[l]),
                               rmsnorm(out_b, out_norm_mla[l])], axis=-1)
        h = h + mix @ w_o[l]
        u = rmsnorm(h, ffn_norm[l])
        h = h + (jax.nn.silu(u @ w_gate[l]) * (u @ w_up[l])) @ w_down[l]

    h = rmsnorm(h, final_norm)
    start = front + N_META
    return h[:, start:start + S]
```

```python
import numpy as np
import jax
import jax.numpy as jnp
from jax import lax
from jax.experimental import pallas as pl
from jax.experimental.pallas import tpu as pltpu

N_META = 16
WINDOW = 128
ROPE_THETA = 10000.0
EPS = 1e-6
NEG = -1e30
SWA_HEADS = 8
SWA_KV_HEADS = 2
SWA_HEAD_DIM = 64
SWA_GROUP = SWA_HEADS // SWA_KV_HEADS
MLA_HEADS = 8
MLA_Q_RANK = 256
MLA_KV_RANK = 128
MLA_NOPE_DIM = 64
MLA_ROPE_DIM = 32
MLA_V_DIM = 64
MLA_QK_DIM = MLA_NOPE_DIM + MLA_ROPE_DIM
SWA_Q_W = SWA_HEADS * SWA_HEAD_DIM
SWA_KV_W = SWA_KV_HEADS * SWA_HEAD_DIM
MLA_OUT_W = MLA_HEADS * MLA_V_DIM
MIX_W = SWA_Q_W + MLA_OUT_W

LANES = 128
BLK = 256
HALF = BLK // 2
MIB = 1024 * 1024

C_QA = 0
C_KA = C_QA + SWA_Q_W
C_VA = C_KA + SWA_KV_W
C_QL = C_VA + SWA_KV_W
C_KVL = C_QL + MLA_Q_RANK
C_KR = C_KVL + MLA_KV_RANK
W1_COLS = C_KR + LANES

T_CAQ, T_SAQ, T_CA, T_SA, T_CBQ, T_SBQ, T_CB, T_SB = range(8)

F32 = jnp.float32
BF16 = jnp.bfloat16


def _rms(x):
    return x * lax.rsqrt(jnp.mean(x * x, axis=-1, keepdims=True) + EPS)


def _nt(a, b):
    return lax.dot_general(a, b, (((1,), (1,)), ((), ())), preferred_element_type=F32)


def _rope(x, c, s):
    return x * c + pltpu.roll(x, LANES // 2, 1) * s


def _pre_kernel(h_ref, tab_ref, gattn_ref, w1_ref, gq_ref, wq_ref, gkv_ref, wkn_ref, wvt_ref,
                qa_ref, ka_ref, vat_ref, qb_ref, kb_ref, vbt_ref):
    def tab(i):
        return tab_ref[:, i * LANES:(i + 1) * LANES]

    u = (_rms(h_ref[0]) * gattn_ref[...]).astype(BF16)
    proj = jnp.dot(u, w1_ref[...], preferred_element_type=F32)

    caq, saq = tab(T_CAQ), tab(T_SAQ)
    for i in range(SWA_Q_W // LANES):
        sl = slice(C_QA + i * LANES, C_QA + (i + 1) * LANES)
        qa_ref[0, :, i * LANES:(i + 1) * LANES] = _rope(proj[:, sl], caq, saq).astype(BF16)
    ka_ref[0] = _rope(proj[:, C_KA:C_KA + LANES], tab(T_CA), tab(T_SA)).astype(BF16)
    vat = proj[:, C_VA:C_VA + LANES].T
    vat_ref[0, 0] = vat[:, :HALF].astype(BF16)
    vat_ref[0, 1] = vat[:, HALF:].astype(BF16)

    qln = (_rms(proj[:, C_QL:C_QL + MLA_Q_RANK]) * gq_ref[...]).astype(BF16)
    qb = jnp.dot(qln, wq_ref[...], preferred_element_type=F32)
    cbq, sbq = tab(T_CBQ), tab(T_SBQ)
    for hd in range(MLA_HEADS):
        sl = slice(hd * LANES, (hd + 1) * LANES)
        qb_ref[0, :, sl] = _rope(qb[:, sl], cbq, sbq).astype(BF16)

    kvn = _rms(proj[:, C_KVL:C_KVL + MLA_KV_RANK]) * gkv_ref[...]
    kn = jnp.dot(kvn.astype(BF16), wkn_ref[...], preferred_element_type=F32)
    kr = _rope(proj[:, C_KR:C_KR + LANES], tab(T_CB), tab(T_SB))
    for hd in range(MLA_HEADS):
        sl = slice(hd * LANES, (hd + 1) * LANES)
        kb_ref[0, :, sl] = (kn[:, sl] + kr).astype(BF16)
    kvn_t = kvn.T.astype(BF16)
    vbt_ref[0, 0] = jnp.dot(wvt_ref[...], kvn_t, preferred_element_type=F32).astype(BF16)


def _make_attn_kernel(front):
    def attn_kernel(sinks_ref, qa_ref, kap_ref, kac_ref, vap_ref, vac_ref, qb_ref, kb_ref,
                    vbt_ref, o_ref, m_s, l_s, acc_s):
        j = pl.program_id(1)

        kwin = jnp.concatenate([kap_ref[0], kac_ref[0]], axis=0)
        lane = lax.broadcasted_iota(jnp.int32, (1, LANES), 1)
        first_kv = (lane % (LANES // 2)) < (SWA_HEAD_DIM // 2)
        zero = jnp.zeros_like(kwin)
        kg = (jnp.where(first_kv, kwin, zero), jnp.where(first_kv, zero, kwin))
        vwin = jnp.concatenate([vap_ref[0, 0], vac_ref[0, 0], vac_ref[0, 1]], axis=1)
        r = lax.broadcasted_iota(jnp.int32, (HALF + BLK, BLK), 0)
        c = lax.broadcasted_iota(jnp.int32, (HALF + BLK, BLK), 1)
        d = c + HALF - r
        valid = (d >= 0) & (d < WINDOW) & (r + (j * BLK - HALF) >= front)
        for i in range(SWA_GROUP):
            qg = qa_ref[0, :, i * LANES:(i + 1) * LANES]
            for g in range(SWA_KV_HEADS):
                hd = i + SWA_GROUP * g
                s = jnp.where(valid, _nt(kg[g], qg), NEG)
                sink = sinks_ref[hd]
                m = jnp.maximum(jnp.max(s, axis=0, keepdims=True), sink)
                p = jnp.exp(s - m)
                l = jnp.sum(p, axis=0, keepdims=True) + jnp.exp(sink - m)
                vg = vwin[g * SWA_HEAD_DIM:(g + 1) * SWA_HEAD_DIM, :]
                o = jnp.dot(vg, p.astype(BF16), preferred_element_type=F32)
                o_ref[0, 0, hd * SWA_HEAD_DIM:(hd + 1) * SWA_HEAD_DIM, :] = (o / l).astype(o_ref.dtype)

        def qh(hd):
            return qb_ref[0, :, hd * LANES:(hd + 1) * LANES]

        for hd in range(MLA_HEADS):
            km = kb_ref[0, front:BLK, hd * LANES:(hd + 1) * LANES]
            s = jnp.where(j >= 1, _nt(km, qh(hd)), NEG)
            m = jnp.max(s, axis=0, keepdims=True)
            p = jnp.exp(s - m)
            m_s[hd:hd + 1, :] = m
            l_s[hd:hd + 1, :] = jnp.sum(p, axis=0, keepdims=True)
            vm = vbt_ref[0, 0, hd * MLA_V_DIM:(hd + 1) * MLA_V_DIM, front:BLK]
            acc_s[hd * MLA_V_DIM:(hd + 1) * MLA_V_DIM, :] = jnp.dot(
                vm, p.astype(BF16), preferred_element_type=F32)

        def step(i, mask):
            off = pl.multiple_of(i * BLK, BLK)
            for hd in range(MLA_HEADS):
                ki = kb_ref[0, pl.ds(off, BLK), hd * LANES:(hd + 1) * LANES]
                s = _nt(ki, qh(hd))
                if mask is not None:
                    s = jnp.where(mask, s, NEG)
                m_old = m_s[hd:hd + 1, :]
                m_new = jnp.maximum(m_old, jnp.max(s, axis=0, keepdims=True))
                alpha = jnp.exp(m_old - m_new)
                p = jnp.exp(s - m_new)
                l_s[hd:hd + 1, :] = l_s[hd:hd + 1, :] * alpha + jnp.sum(p, axis=0, keepdims=True)
                m_s[hd:hd + 1, :] = m_new
                rows = slice(hd * MLA_V_DIM, (hd + 1) * MLA_V_DIM)
                vi = vbt_ref[0, i, rows, :]
                acc_s[rows, :] = acc_s[rows, :] * alpha + jnp.dot(
                    vi, p.astype(BF16), preferred_element_type=F32)

        def body(i, carry):
            step(i, None)
            return carry

        lax.fori_loop(1, j, body, 0)
        rr = lax.broadcasted_iota(jnp.int32, (BLK, BLK), 0)
        cc = lax.broadcasted_iota(jnp.int32, (BLK, BLK), 1)
        step(j, (rr <= cc) & (rr + j * BLK >= front))

        for hd in range(MLA_HEADS):
            rows = slice(hd * MLA_V_DIM, (hd + 1) * MLA_V_DIM)
            o = acc_s[rows, :] / l_s[hd:hd + 1, :]
            o_ref[0, 0, SWA_Q_W + hd * MLA_V_DIM:SWA_Q_W + (hd + 1) * MLA_V_DIM, :] = o.astype(o_ref.dtype)

    return attn_kernel


def _post_kernel(h_ref, o_ref, gmix_ref, wo_ref, gffn_ref, wg_ref, wu_ref, wd_ref, out_ref):
    a = o_ref[0, 0].astype(F32)
    aa, bb = a[:SWA_Q_W], a[SWA_Q_W:]
    ia = lax.rsqrt(jnp.mean(aa * aa, axis=0, keepdims=True) + EPS)
    ib = lax.rsqrt(jnp.mean(bb * bb, axis=0, keepdims=True) + EPS)
    y = jnp.concatenate([aa * ia, bb * ib], axis=0)
    mix = (y.T * gmix_ref[...]).astype(BF16)
    h1 = h_ref[0] + jnp.dot(mix, wo_ref[...], preferred_element_type=F32)
    u = (_rms(h1) * gffn_ref[...]).astype(BF16)
    g = jnp.dot(u, wg_ref[...], preferred_element_type=F32)
    up = jnp.dot(u, wu_ref[...], preferred_element_type=F32)
    act = (g * jax.nn.sigmoid(g) * up).astype(BF16)
    out_ref[0] = h1 + jnp.dot(act, wd_ref[...], preferred_element_type=F32)


def _final_kernel(h_ref, g_ref, out_ref):
    out_ref[0] = _rms(h_ref[0]) * g_ref[...]


def _gather_cols(w, idx):
    idx = np.asarray(idx)
    cols = jnp.take(w, jnp.asarray(np.maximum(idx, 0)), axis=-1)
    return jnp.where(jnp.asarray(idx >= 0), cols, 0.0)


def _w1_index(o1, o2, o3, o4, o5):
    half = SWA_HEAD_DIM // 2
    idx = []
    for i in range(SWA_GROUP):
        pair = (i, i + SWA_GROUP)
        for part in range(2):
            for hd in pair:
                idx += [hd * SWA_HEAD_DIM + part * half + t for t in range(half)]
    for part in range(2):
        for g in range(SWA_KV_HEADS):
            idx += [o1 + g * SWA_HEAD_DIM + part * half + t for t in range(half)]
    idx += list(range(o2, o5))
    idx += _mla_group_index(o5, None)
    return idx


def _mla_group_index(rope_base, nope_base):
    hn, hr = MLA_NOPE_DIM // 2, MLA_ROPE_DIM // 2
    pad = LANES // 2 - hn - hr
    idx = []
    for part in range(2):
        idx += [-1] * hn if nope_base is None else [nope_base + part * hn + t for t in range(hn)]
        idx += [-1] * hr if rope_base is None else [rope_base + part * hr + t for t in range(hr)]
        idx += [-1] * pad
    return idx


def _rope_tables(tp, front):
    pos = (jnp.arange(tp) - front).astype(F32)

    def cs(d):
        inv = ROPE_THETA ** (-jnp.arange(0, d, 2, dtype=F32) / d)
        ang = pos[:, None] * inv[None, :]
        return jnp.cos(ang), jnp.sin(ang)

    ca, sa = cs(SWA_HEAD_DIM)
    CA = jnp.concatenate([ca] * 4, axis=1)
    SA = jnp.concatenate([-sa, -sa, sa, sa], axis=1)
    cb, sb = cs(MLA_ROPE_DIM)
    hn, hr = MLA_NOPE_DIM // 2, MLA_ROPE_DIM // 2
    pad = LANES // 2 - hn - hr
    one_n, one_p = jnp.ones((tp, hn), F32), jnp.ones((tp, pad), F32)
    z_n, z_p = jnp.zeros((tp, hn), F32), jnp.zeros((tp, pad), F32)
    CB = jnp.concatenate([one_n, cb, one_p, one_n, cb, one_p], axis=1)
    SB = jnp.concatenate([z_n, -sb, z_p, z_n, sb, z_p], axis=1)
    sa_scale = SWA_HEAD_DIM ** -0.5
    sb_scale = MLA_QK_DIM ** -0.5
    return jnp.concatenate([CA * sa_scale, SA * sa_scale, CA, SA,
                            CB * sb_scale, SB * sb_scale, CB, SB], axis=1)


def _const_spec(shape):
    return pl.BlockSpec(shape, lambda *_: (0,) * len(shape), pipeline_mode=pl.Buffered(1))


def kernel(x, meta_tokens, attn_norm, w_in, q_norm, w_q_up, kv_norm, w_kv_up, sinks,
           out_norm_swa, out_norm_mla, w_o, ffn_norm, w_gate, w_up, w_down, final_norm):
    B, S, D = x.shape
    depth = w_in.shape[0]
    d_ff = w_gate.shape[-1]
    assert S % BLK == 0 and N_META <= BLK and WINDOW <= HALF
    front = BLK - N_META
    tp = BLK + S
    nblk = tp // BLK

    o1 = SWA_Q_W
    o2 = o1 + SWA_KV_W
    o3 = o2 + SWA_KV_W
    o4 = o3 + MLA_Q_RANK
    o5 = o4 + MLA_KV_RANK

    w1 = _gather_cols(w_in, _w1_index(o1, o2, o3, o4, o5)).astype(BF16)
    wq_idx, wkn_idx, wv_idx = [], [], []
    for hd in range(MLA_HEADS):
        wq_idx += _mla_group_index(hd * MLA_QK_DIM + MLA_NOPE_DIM, hd * MLA_QK_DIM)
        wkn_idx += _mla_group_index(None, hd * (MLA_NOPE_DIM + MLA_V_DIM))
        wv_idx += [hd * (MLA_NOPE_DIM + MLA_V_DIM) + MLA_NOPE_DIM + t for t in range(MLA_V_DIM)]
    wq = _gather_cols(w_q_up, wq_idx).astype(BF16)
    wkn = _gather_cols(w_kv_up, wkn_idx).astype(BF16)
    wvt = jnp.swapaxes(_gather_cols(w_kv_up, wv_idx), 1, 2).astype(BF16)
    wo = w_o.astype(BF16)
    wg = w_gate.astype(BF16)
    wu = w_up.astype(BF16)
    wd = w_down.astype(BF16)
    gmix = jnp.concatenate([out_norm_swa, out_norm_mla], axis=-1)
    tabs = _rope_tables(tp, front)

    h = jnp.concatenate([jnp.zeros((B, front, D), x.dtype),
                         jnp.broadcast_to(meta_tokens.astype(x.dtype)[None], (B, N_META, D)),
                         x], axis=1)

    hw = MLA_HEADS * LANES
    pre_call = pl.pallas_call(
        _pre_kernel,
        grid=(nblk, B),
        in_specs=[
            pl.BlockSpec((1, BLK, D), lambda j, b: (b, j, 0)),
            pl.BlockSpec((BLK, 8 * LANES), lambda j, b: (j, 0)),
            _const_spec((1, D)),
            _const_spec((D, W1_COLS)),
            _const_spec((1, MLA_Q_RANK)),
            _const_spec((MLA_Q_RANK, hw)),
            _const_spec((1, MLA_KV_RANK)),
            _const_spec((MLA_KV_RANK, hw)),
            _const_spec((MLA_OUT_W, MLA_KV_RANK)),
        ],
        out_specs=[
            pl.BlockSpec((1, BLK, SWA_Q_W), lambda j, b: (b, j, 0)),
            pl.BlockSpec((1, BLK, LANES), lambda j, b: (b, j, 0)),
            pl.BlockSpec((1, 2, LANES, HALF), lambda j, b: (b, j, 0, 0)),
            pl.BlockSpec((1, BLK, hw), lambda j, b: (b, j, 0)),
            pl.BlockSpec((1, BLK, hw), lambda j, b: (b, j, 0)),
            pl.BlockSpec((1, 1, MLA_OUT_W, BLK), lambda j, b: (b, j, 0, 0)),
        ],
        out_shape=[
            jax.ShapeDtypeStruct((B, tp, SWA_Q_W), BF16),
            jax.ShapeDtypeStruct((B, tp, LANES), BF16),
            jax.ShapeDtypeStruct((B, 2 * nblk, LANES, HALF), BF16),
            jax.ShapeDtypeStruct((B, tp, hw), BF16),
            jax.ShapeDtypeStruct((B, tp, hw), BF16),
            jax.ShapeDtypeStruct((B, nblk, MLA_OUT_W, BLK), BF16),
        ],
        compiler_params=pltpu.CompilerParams(
            dimension_semantics=("arbitrary", "arbitrary"), vmem_limit_bytes=40 * MIB),
        name="pre_attn",
    )

    attn_call = pl.pallas_call(
        _make_attn_kernel(front),
        grid=(B, nblk),
        in_specs=[
            pl.BlockSpec(memory_space=pltpu.SMEM),
            pl.BlockSpec((1, BLK, SWA_Q_W), lambda b, j: (b, j, 0)),
            pl.BlockSpec((1, HALF, LANES), lambda b, j: (b, jnp.maximum(2 * j - 1, 0), 0)),
            pl.BlockSpec((1, BLK, LANES), lambda b, j: (b, j, 0)),
            pl.BlockSpec((1, 1, LANES, HALF), lambda b, j: (b, jnp.maximum(2 * j - 1, 0), 0, 0)),
            pl.BlockSpec((1, 2, LANES, HALF), lambda b, j: (b, j, 0, 0)),
            pl.BlockSpec((1, BLK, hw), lambda b, j: (b, j, 0)),
            pl.BlockSpec((1, tp, hw), lambda b, j: (b, 0, 0)),
            pl.BlockSpec((1, nblk, MLA_OUT_W, BLK), lambda b, j: (b, 0, 0, 0)),
        ],
        out_specs=pl.BlockSpec((1, 1, MIX_W, BLK), lambda b, j: (b, j, 0, 0)),
        out_shape=jax.ShapeDtypeStruct((B, nblk, MIX_W, BLK), BF16),
        scratch_shapes=[
            pltpu.VMEM((MLA_HEADS, BLK), F32),
            pltpu.VMEM((MLA_HEADS, BLK), F32),
            pltpu.VMEM((MLA_OUT_W, BLK), F32),
        ],
        compiler_params=pltpu.CompilerParams(
            dimension_semantics=("arbitrary", "arbitrary"), vmem_limit_bytes=48 * MIB),
        name="attn",
    )

    post_call = pl.pallas_call(
        _post_kernel,
        grid=(B, nblk),
        in_specs=[
            pl.BlockSpec((1, BLK, D), lambda b, j: (b, j, 0)),
            pl.BlockSpec((1, 1, MIX_W, BLK), lambda b, j: (b, j, 0, 0)),
            _const_spec((1, MIX_W)),
            _const_spec((MIX_W, D)),
            _const_spec((1, D)),
            _const_spec((D, d_ff)),
            _const_spec((D, d_ff)),
            _const_spec((d_ff, D)),
        ],
        out_specs=pl.BlockSpec((1, BLK, D), lambda b, j: (b, j, 0)),
        out_shape=jax.ShapeDtypeStruct((B, tp, D), F32),
        compiler_params=pltpu.CompilerParams(
            dimension_semantics=("arbitrary", "arbitrary"), vmem_limit_bytes=56 * MIB),
        name="post_attn",
    )

    for l in range(depth):
        qa, ka, vat, qb, kb, vbt = pre_call(
            h, tabs, attn_norm[l][None], w1[l], q_norm[l][None], wq[l], kv_norm[l][None],
            wkn[l], wvt[l])
        o_t = attn_call(sinks[l], qa, ka, ka, vat, vat, qb, kb, vbt)
        h = post_call(h, o_t, gmix[l][None], wo[l], ffn_norm[l][None], wg[l], wu[l], wd[l])

    return pl.pallas_call(
        _final_kernel,
        grid=(B, S // BLK),
        in_specs=[
            pl.BlockSpec((1, BLK, D), lambda b, j: (b, j + 1, 0)),
            _const_spec((1, D)),
        ],
        out_specs=pl.BlockSpec((1, BLK, D), lambda b, j: (b, j, 0)),
        out_shape=jax.ShapeDtypeStruct((B, S, D), x.dtype),
        compiler_params=pltpu.CompilerParams(dimension_semantics=("arbitrary", "arbitrary")),
        name="final_norm",
    )(h, final_norm[None])
```

```python
import numpy as np
import jax
import jax.numpy as jnp
from jax import lax
from jax.experimental import pallas as pl
from jax.experimental.pallas import tpu as pltpu

N_META = 16
WINDOW = 128
ROPE_THETA = 10000.0
EPS = 1e-6
NEG = -1e30
SWA_HEADS = 8
SWA_KV_HEADS = 2
SWA_HEAD_DIM = 64
SWA_GROUP = SWA_HEADS // SWA_KV_HEADS
MLA_HEADS = 8
MLA_Q_RANK = 256
MLA_KV_RANK = 128
MLA_NOPE_DIM = 64
MLA_ROPE_DIM = 32
MLA_V_DIM = 64
MLA_QK_DIM = MLA_NOPE_DIM + MLA_ROPE_DIM
SWA_Q_W = SWA_HEADS * SWA_HEAD_DIM
SWA_KV_W = SWA_KV_HEADS * SWA_HEAD_DIM
MLA_OUT_W = MLA_HEADS * MLA_V_DIM
MIX_W = SWA_Q_W + MLA_OUT_W

LANES = 128
BLK = 256
HALF = BLK // 2
MIB = 1024 * 1024
PIPE_DEPTH = 3
S_SLOTS = PIPE_DEPTH + 1
ONES_ROWS = 16
ACC_ROWS = MLA_V_DIM + ONES_ROWS
LOG2E = 1.4426950408889634

C_QA = 0
C_KA = C_QA + SWA_Q_W
C_VA = C_KA + SWA_KV_W
C_QL = C_VA + SWA_KV_W
C_KVL = C_QL + MLA_Q_RANK
C_KR = C_KVL + MLA_KV_RANK
W1_COLS = C_KR + LANES

T_CAQ, T_SAQ, T_CA, T_SA, T_CBQ, T_SBQ, T_CB, T_SB = range(8)

F32 = jnp.float32
BF16 = jnp.bfloat16


def _rms(x):
    return x * lax.rsqrt(jnp.mean(x * x, axis=-1, keepdims=True) + EPS)


def _nt(a, b):
    return lax.dot_general(a, b, (((1,), (1,)), ((), ())), preferred_element_type=F32)


def _rope(x, c, s):
    return x * c + pltpu.roll(x, LANES // 2, 1) * s


def _pre_kernel(h_ref, tab_ref, gattn_ref, w1_ref, gq_ref, wq_ref, gkv_ref, wkn_ref, wvt_ref,
                qa_ref, ka_ref, vat_ref, qb_ref, kb_ref, vbt_ref):
    def tab(i):
        return tab_ref[:, i * LANES:(i + 1) * LANES]

    u = (_rms(h_ref[0]) * gattn_ref[...]).astype(BF16)
    proj = jnp.dot(u, w1_ref[...], preferred_element_type=F32)

    caq, saq = tab(T_CAQ), tab(T_SAQ)
    for i in range(SWA_Q_W // LANES):
        sl = slice(C_QA + i * LANES, C_QA + (i + 1) * LANES)
        qa_ref[0, :, i * LANES:(i + 1) * LANES] = _rope(proj[:, sl], caq, saq).astype(BF16)
    ka_ref[0] = _rope(proj[:, C_KA:C_KA + LANES], tab(T_CA), tab(T_SA)).astype(BF16)
    vat = proj[:, C_VA:C_VA + LANES].T
    vat_ref[0, 0] = vat[:, :HALF].astype(BF16)
    vat_ref[0, 1] = vat[:, HALF:].astype(BF16)

    qln = (_rms(proj[:, C_QL:C_QL + MLA_Q_RANK]) * gq_ref[...]).astype(BF16)
    qb = jnp.dot(qln, wq_ref[...], preferred_element_type=F32)
    cbq, sbq = tab(T_CBQ), tab(T_SBQ)
    for hd in range(MLA_HEADS):
        sl = slice(hd * LANES, (hd + 1) * LANES)
        qb_ref[0, :, sl] = _rope(qb[:, sl], cbq, sbq).astype(BF16)

    kvn = _rms(proj[:, C_KVL:C_KVL + MLA_KV_RANK]) * gkv_ref[...]
    kn = jnp.dot(kvn.astype(BF16), wkn_ref[...], preferred_element_type=F32)
    kr = _rope(proj[:, C_KR:C_KR + LANES], tab(T_CB), tab(T_SB))
    for hd in range(MLA_HEADS):
        sl = slice(hd * LANES, (hd + 1) * LANES)
        kb_ref[0, :, sl] = (kn[:, sl] + kr).astype(BF16)
    kvn_t = kvn.T.astype(BF16)
    vbt_ref[0, 0] = jnp.dot(wvt_ref[...], kvn_t, preferred_element_type=F32).astype(BF16)


def _run_pipelined(tasks, depth):
    pending = []
    for score_fn, finish_fn in tasks:
        pending.append((finish_fn, score_fn()))
        if len(pending) > depth:
            fin, s = pending.pop(0)
            fin(s)
    for fin, s in pending:
        fin(s)


def _make_attn_kernel(front):
    def attn_kernel(sinks_ref, qa_ref, kap_ref, kac_ref, vap_ref, vac_ref, qb_ref, kb_ref,
                    vbt_ref, o_ref, m_s, acc_s, s_buf):
        j = pl.program_id(1)

        def with_ones(v):
            return jnp.concatenate([v, jnp.ones((ONES_ROWS, v.shape[1]), v.dtype)], axis=0)

        kwin = jnp.concatenate([kap_ref[0], kac_ref[0]], axis=0)
        lane = lax.broadcasted_iota(jnp.int32, (1, LANES), 1)
        first_kv = (lane % (LANES // 2)) < (SWA_HEAD_DIM // 2)
        zero = jnp.zeros_like(kwin)
        kg = (jnp.where(first_kv, kwin, zero), jnp.where(first_kv, zero, kwin))
        vwin = jnp.concatenate([vap_ref[0, 0], vac_ref[0, 0], vac_ref[0, 1]], axis=1)
        vg_aug = [with_ones(vwin[g * SWA_HEAD_DIM:(g + 1) * SWA_HEAD_DIM, :])
                  for g in range(SWA_KV_HEADS)]
        r = lax.broadcasted_iota(jnp.int32, (HALF + BLK, BLK), 0)
        c = lax.broadcasted_iota(jnp.int32, (HALF + BLK, BLK), 1)
        d = c + HALF - r
        valid = (d >= 0) & (d < WINDOW) & (r + (j * BLK - HALF) >= front)

        def swa_task(i, g):
            hd = i + SWA_GROUP * g

            def score():
                return _nt(kg[g], qa_ref[0, :, i * LANES:(i + 1) * LANES])

            def finish(s):
                s = jnp.where(valid, s, NEG)
                sink = sinks_ref[hd] * LOG2E
                m = jnp.maximum(jnp.max(s, axis=0, keepdims=True), sink)
                p = jnp.exp2(s - m).astype(BF16)
                o = jnp.dot(vg_aug[g], p, preferred_element_type=F32)
                l = o[SWA_HEAD_DIM:SWA_HEAD_DIM + 1] + jnp.exp2(sink - m)
                o_ref[0, 0, hd * SWA_HEAD_DIM:(hd + 1) * SWA_HEAD_DIM, :] = (
                    o[:SWA_HEAD_DIM] * (1.0 / l)).astype(o_ref.dtype)

            return score, finish

        def qh(hd):
            return qb_ref[0, :, hd * LANES:(hd + 1) * LANES]

        rr = lax.broadcasted_iota(jnp.int32, (BLK, BLK), 0)
        cc = lax.broadcasted_iota(jnp.int32, (BLK, BLK), 1)
        diag_mask = (rr <= cc) & (rr + j * BLK >= front)

        def v_rows(hd):
            return slice(hd * MLA_V_DIM, (hd + 1) * MLA_V_DIM)

        def acc_rows(hd):
            return slice(hd * ACC_ROWS, (hd + 1) * ACC_ROWS)

        def first_task(hd):
            def score():
                kd = kb_ref[0, pl.ds(pl.multiple_of(j * BLK, BLK), BLK), hd * LANES:(hd + 1) * LANES]
                km = kb_ref[0, front:BLK, hd * LANES:(hd + 1) * LANES]
                return _nt(kd, qh(hd)), _nt(km, qh(hd))

            def finish(ss):
                sd = jnp.where(diag_mask, ss[0], NEG)
                sm = jnp.where(j >= 1, ss[1], NEG)
                m = jnp.maximum(jnp.max(sd, axis=0, keepdims=True), jnp.max(sm, axis=0, keepdims=True))
                pd = jnp.exp2(sd - m).astype(BF16)
                pm = jnp.exp2(sm - m).astype(BF16)
                m_s[hd:hd + 1, :] = m
                vd = with_ones(vbt_ref[0, j, v_rows(hd), :])
                vm = with_ones(vbt_ref[0, 0, v_rows(hd), front:BLK])
                acc_s[acc_rows(hd), :] = (jnp.dot(vd, pd, preferred_element_type=F32)
                                          + jnp.dot(vm, pm, preferred_element_type=F32))

            return score, finish

        def block_score(i, hd):
            ki = kb_ref[0, pl.ds(pl.multiple_of(i * BLK, BLK), BLK), hd * LANES:(hd + 1) * LANES]
            s_buf[hd % S_SLOTS] = _nt(ki, qh(hd))

        def block_finish(i, hd):
            s = s_buf[hd % S_SLOTS]
            m_old = m_s[hd:hd + 1, :]
            m_new = jnp.maximum(m_old, jnp.max(s, axis=0, keepdims=True))
            alpha = jnp.exp2(m_old - m_new)
            p = jnp.exp2(s - m_new).astype(BF16)
            m_s[hd:hd + 1, :] = m_new
            vi = with_ones(vbt_ref[0, i, v_rows(hd), :])
            acc_s[acc_rows(hd), :] = acc_s[acc_rows(hd), :] * alpha + jnp.dot(
                vi, p, preferred_element_type=F32)

        tasks = [swa_task(i, g) for i in range(SWA_GROUP) for g in range(SWA_KV_HEADS)]
        tasks += [first_task(hd) for hd in range(MLA_HEADS)]
        _run_pipelined(tasks, PIPE_DEPTH)

        for hd in range(PIPE_DEPTH):
            block_score(1, hd)

        def body(i, carry):
            for hd in range(MLA_HEADS):
                nxt = hd + PIPE_DEPTH
                if nxt < MLA_HEADS:
                    block_score(i, nxt)
                else:
                    block_score(i + 1, nxt - MLA_HEADS)
                block_finish(i, hd)
            return carry

        lax.fori_loop(1, j, body, 0)

        for hd in range(MLA_HEADS):
            acc = acc_s[acc_rows(hd), :]
            o = acc[:MLA_V_DIM] * (1.0 / acc[MLA_V_DIM:MLA_V_DIM + 1])
            o_ref[0, 0, SWA_Q_W + hd * MLA_V_DIM:SWA_Q_W + (hd + 1) * MLA_V_DIM, :] = o.astype(o_ref.dtype)

    return attn_kernel


def _post_kernel(h_ref, o_ref, gmix_ref, wo_ref, gffn_ref, wg_ref, wu_ref, wd_ref, out_ref):
    a = o_ref[0, 0].astype(F32)
    aa, bb = a[:SWA_Q_W], a[SWA_Q_W:]
    ia = lax.rsqrt(jnp.mean(aa * aa, axis=0, keepdims=True) + EPS)
    ib = lax.rsqrt(jnp.mean(bb * bb, axis=0, keepdims=True) + EPS)
    y = jnp.concatenate([aa * ia, bb * ib], axis=0)
    mix = (y.T * gmix_ref[...]).astype(BF16)
    h1 = h_ref[0] + jnp.dot(mix, wo_ref[...], preferred_element_type=F32)
    u = (_rms(h1) * gffn_ref[...]).astype(BF16)
    g = jnp.dot(u, wg_ref[...], preferred_element_type=F32)
    up = jnp.dot(u, wu_ref[...], preferred_element_type=F32)
    act = (g * jax.nn.sigmoid(g) * up).astype(BF16)
    out_ref[0] = h1 + jnp.dot(act, wd_ref[...], preferred_element_type=F32)


def _final_kernel(h_ref, g_ref, out_ref):
    out_ref[0] = _rms(h_ref[0]) * g_ref[...]


def _gather_cols(w, idx):
    idx = np.asarray(idx)
    cols = jnp.take(w, jnp.asarray(np.maximum(idx, 0)), axis=-1)
    return jnp.where(jnp.asarray(idx >= 0), cols, 0.0)


def _w1_index(o1, o2, o3, o4, o5):
    half = SWA_HEAD_DIM // 2
    idx = []
    for i in range(SWA_GROUP):
        pair = (i, i + SWA_GROUP)
        for part in range(2):
            for hd in pair:
                idx += [hd * SWA_HEAD_DIM + part * half + t for t in range(half)]
    for part in range(2):
        for g in range(SWA_KV_HEADS):
            idx += [o1 + g * SWA_HEAD_DIM + part * half + t for t in range(half)]
    idx += list(range(o2, o5))
    idx += _mla_group_index(o5, None)
    return idx


def _mla_group_index(rope_base, nope_base):
    hn, hr = MLA_NOPE_DIM // 2, MLA_ROPE_DIM // 2
    pad = LANES // 2 - hn - hr
    idx = []
    for part in range(2):
        idx += [-1] * hn if nope_base is None else [nope_base + part * hn + t for t in range(hn)]
        idx += [-1] * hr if rope_base is None else [rope_base + part * hr + t for t in range(hr)]
        idx += [-1] * pad
    return idx


def _rope_tables(tp, front):
    pos = (jnp.arange(tp) - front).astype(F32)

    def cs(d):
        inv = ROPE_THETA ** (-jnp.arange(0, d, 2, dtype=F32) / d)
        ang = pos[:, None] * inv[None, :]
        return jnp.cos(ang), jnp.sin(ang)

    ca, sa = cs(SWA_HEAD_DIM)
    CA = jnp.concatenate([ca] * 4, axis=1)
    SA = jnp.concatenate([-sa, -sa, sa, sa], axis=1)
    cb, sb = cs(MLA_ROPE_DIM)
    hn, hr = MLA_NOPE_DIM // 2, MLA_ROPE_DIM // 2
    pad = LANES // 2 - hn - hr
    one_n, one_p = jnp.ones((tp, hn), F32), jnp.ones((tp, pad), F32)
    z_n, z_p = jnp.zeros((tp, hn), F32), jnp.zeros((tp, pad), F32)
    CB = jnp.concatenate([one_n, cb, one_p, one_n, cb, one_p], axis=1)
    SB = jnp.concatenate([z_n, -sb, z_p, z_n, sb, z_p], axis=1)
    sa_scale = SWA_HEAD_DIM ** -0.5 * LOG2E
    sb_scale = MLA_QK_DIM ** -0.5 * LOG2E
    return jnp.concatenate([CA * sa_scale, SA * sa_scale, CA, SA,
                            CB * sb_scale, SB * sb_scale, CB, SB], axis=1)


def _const_spec(shape):
    return pl.BlockSpec(shape, lambda *_: (0,) * len(shape), pipeline_mode=pl.Buffered(1))


def kernel(x, meta_tokens, attn_norm, w_in, q_norm, w_q_up, kv_norm, w_kv_up, sinks,
           out_norm_swa, out_norm_mla, w_o, ffn_norm, w_gate, w_up, w_down, final_norm):
    B, S, D = x.shape
    depth = w_in.shape[0]
    d_ff = w_gate.shape[-1]
    assert S % BLK == 0 and N_META <= BLK and WINDOW <= HALF and MLA_HEADS % S_SLOTS == 0
    front = BLK - N_META
    tp = BLK + S
    nblk = tp // BLK

    o1 = SWA_Q_W
    o2 = o1 + SWA_KV_W
    o3 = o2 + SWA_KV_W
    o4 = o3 + MLA_Q_RANK
    o5 = o4 + MLA_KV_RANK

    w1 = _gather_cols(w_in, _w1_index(o1, o2, o3, o4, o5)).astype(BF16)
    wq_idx, wkn_idx, wv_idx = [], [], []
    for hd in range(MLA_HEADS):
        wq_idx += _mla_group_index(hd * MLA_QK_DIM + MLA_NOPE_DIM, hd * MLA_QK_DIM)
        wkn_idx += _mla_group_index(None, hd * (MLA_NOPE_DIM + MLA_V_DIM))
        wv_idx += [hd * (MLA_NOPE_DIM + MLA_V_DIM) + MLA_NOPE_DIM + t for t in range(MLA_V_DIM)]
    wq = _gather_cols(w_q_up, wq_idx).astype(BF16)
    wkn = _gather_cols(w_kv_up, wkn_idx).astype(BF16)
    wvt = jnp.swapaxes(_gather_cols(w_kv_up, wv_idx), 1, 2).astype(BF16)
    wo = w_o.astype(BF16)
    wg = w_gate.astype(BF16)
    wu = w_up.astype(BF16)
    wd = w_down.astype(BF16)
    gmix = jnp.concatenate([out_norm_swa, out_norm_mla], axis=-1)
    tabs = _rope_tables(tp, front)

    h = jnp.concatenate([jnp.zeros((B, front, D), x.dtype),
                         jnp.broadcast_to(meta_tokens.astype(x.dtype)[None], (B, N_META, D)),
                         x], axis=1)

    hw = MLA_HEADS * LANES
    pre_call = pl.pallas_call(
        _pre_kernel,
        grid=(nblk, B),
        in_specs=[
            pl.BlockSpec((1, BLK, D), lambda j, b: (b, j, 0)),
            pl.BlockSpec((BLK, 8 * LANES), lambda j, b: (j, 0)),
            _const_spec((1, D)),
            _const_spec((D, W1_COLS)),
            _const_spec((1, MLA_Q_RANK)),
            _const_spec((MLA_Q_RANK, hw)),
            _const_spec((1, MLA_KV_RANK)),
            _const_spec((MLA_KV_RANK, hw)),
            _const_spec((MLA_OUT_W, MLA_KV_RANK)),
        ],
        out_specs=[
            pl.BlockSpec((1, BLK, SWA_Q_W), lambda j, b: (b, j, 0)),
            pl.BlockSpec((1, BLK, LANES), lambda j, b: (b, j, 0)),
            pl.BlockSpec((1, 2, LANES, HALF), lambda j, b: (b, j, 0, 0)),
            pl.BlockSpec((1, BLK, hw), lambda j, b: (b, j, 0)),
            pl.BlockSpec((1, BLK, hw), lambda j, b: (b, j, 0)),
            pl.BlockSpec((1, 1, MLA_OUT_W, BLK), lambda j, b: (b, j, 0, 0)),
        ],
        out_shape=[
            jax.ShapeDtypeStruct((B, tp, SWA_Q_W), BF16),
            jax.ShapeDtypeStruct((B, tp, LANES), BF16),
            jax.ShapeDtypeStruct((B, 2 * nblk, LANES, HALF), BF16),
            jax.ShapeDtypeStruct((B, tp, hw), BF16),
            jax.ShapeDtypeStruct((B, tp, hw), BF16),
            jax.ShapeDtypeStruct((B, nblk, MLA_OUT_W, BLK), BF16),
        ],
        compiler_params=pltpu.CompilerParams(
            dimension_semantics=("arbitrary", "arbitrary"), vmem_limit_bytes=40 * MIB),
        name="pre_attn",
    )

    attn_call = pl.pallas_call(
        _make_attn_kernel(front),
        grid=(B, nblk),
        in_specs=[
            pl.BlockSpec(memory_space=pltpu.SMEM),
            pl.BlockSpec((1, BLK, SWA_Q_W), lambda b, j: (b, j, 0)),
            pl.BlockSpec((1, HALF, LANES), lambda b, j: (b, jnp.maximum(2 * j - 1, 0), 0)),
            pl.BlockSpec((1, BLK, LANES), lambda b, j: (b, j, 0)),
            pl.BlockSpec((1, 1, LANES, HALF), lambda b, j: (b, jnp.maximum(2 * j - 1, 0), 0, 0)),
            pl.BlockSpec((1, 2, LANES, HALF), lambda b, j: (b, j, 0, 0)),
            pl.BlockSpec((1, BLK, hw), lambda b, j: (b, j, 0)),
            pl.BlockSpec((1, tp, hw), lambda b, j: (b, 0, 0)),
            pl.BlockSpec((1, nblk, MLA_OUT_W, BLK), lambda b, j: (b, 0, 0, 0)),
        ],
        out_specs=pl.BlockSpec((1, 1, MIX_W, BLK), lambda b, j: (b, j, 0, 0)),
        out_shape=jax.ShapeDtypeStruct((B, nblk, MIX_W, BLK), BF16),
        scratch_shapes=[
            pltpu.VMEM((MLA_HEADS, BLK), F32),
            pltpu.VMEM((MLA_HEADS * ACC_ROWS, BLK), F32),
            pltpu.VMEM((S_SLOTS, BLK, BLK), F32),
        ],
        compiler_params=pltpu.CompilerParams(
            dimension_semantics=("arbitrary", "arbitrary"), vmem_limit_bytes=48 * MIB),
        name="attn",
    )

    post_call = pl.pallas_call(
        _post_kernel,
        grid=(B, nblk),
        in_specs=[
            pl.BlockSpec((1, BLK, D), lambda b, j: (b, j, 0)),
            pl.BlockSpec((1, 1, MIX_W, BLK), lambda b, j: (b, j, 0, 0)),
            _const_spec((1, MIX_W)),
            _const_spec((MIX_W, D)),
            _const_spec((1, D)),
            _const_spec((D, d_ff)),
            _const_spec((D, d_ff)),
            _const_spec((d_ff, D)),
        ],
        out_specs=pl.BlockSpec((1, BLK, D), lambda b, j: (b, j, 0)),
        out_shape=jax.ShapeDtypeStruct((B, tp, D), F32),
        compiler_params=pltpu.CompilerParams(
            dimension_semantics=("arbitrary", "arbitrary"), vmem_limit_bytes=56 * MIB),
        name="post_attn",
    )

    for l in range(depth):
        qa, ka, vat, qb, kb, vbt = pre_call(
            h, tabs, attn_norm[l][None], w1[l], q_norm[l][None], wq[l], kv_norm[l][None],
            wkn[l], wvt[l])
        o_t = attn_call(sinks[l], qa, ka, ka, vat, vat, qb, kb, vbt)
        h = post_call(h, o_t, gmix[l][None], wo[l], ffn_norm[l][None], wg[l], wu[l], wd[l])

    return pl.pallas_call(
        _final_kernel,
        grid=(B, S // BLK),
        in_specs=[
            pl.BlockSpec((1, BLK, D), lambda b, j: (b, j + 1, 0)),
            _const_spec((1, D)),
        ],
        out_specs=pl.BlockSpec((1, BLK, D), lambda b, j: (b, j, 0)),
        out_shape=jax.ShapeDtypeStruct((B, S, D), x.dtype),
        compiler_params=pltpu.CompilerParams(dimension_semantics=("arbitrary", "arbitrary")),
        name="final_norm",
    )(h, final_norm[None])
```

```python
import functools

import numpy as np
import jax
import jax.numpy as jnp
from jax import lax
from jax.experimental import pallas as pl
from jax.experimental.pallas import tpu as pltpu

N_META = 16
WINDOW = 128
ROPE_THETA = 10000.0
EPS = 1e-6
NEG = -1e30
SWA_HEADS = 8
SWA_KV_HEADS = 2
SWA_HEAD_DIM = 64
SWA_GROUP = SWA_HEADS // SWA_KV_HEADS
MLA_HEADS = 8
MLA_Q_RANK = 256
MLA_KV_RANK = 128
MLA_NOPE_DIM = 64
MLA_ROPE_DIM = 32
MLA_V_DIM = 64
MLA_QK_DIM = MLA_NOPE_DIM + MLA_ROPE_DIM
SWA_Q_W = SWA_HEADS * SWA_HEAD_DIM
SWA_KV_W = SWA_KV_HEADS * SWA_HEAD_DIM
MLA_OUT_W = MLA_HEADS * MLA_V_DIM
MIX_W = SWA_Q_W + MLA_OUT_W

LANES = 128
BLK = 256
HALF = BLK // 2
MIB = 1024 * 1024
PIPE_DEPTH = 3
S_SLOTS = PIPE_DEPTH + 1
ONES_ROWS = 16
ACC_ROWS = MLA_V_DIM + ONES_ROWS
LOG2E = 1.4426950408889634

C_QA = 0
C_KA = C_QA + SWA_Q_W
C_VA = C_KA + SWA_KV_W
C_QL = C_VA + SWA_KV_W
C_KVL = C_QL + MLA_Q_RANK
C_KR = C_KVL + MLA_KV_RANK
W1_COLS = C_KR + LANES

T_CAQ, T_SAQ, T_CA, T_SA, T_CBQ, T_SBQ, T_CB, T_SB = range(8)

F32 = jnp.float32
BF16 = jnp.bfloat16


def _rms(x):
    return x * lax.rsqrt(jnp.mean(x * x, axis=-1, keepdims=True) + EPS)


def _nt(a, b):
    return lax.dot_general(a, b, (((1,), (1,)), ((), ())), preferred_element_type=F32)


def _rope(x, c, s):
    return x * c + pltpu.roll(x, LANES // 2, 1) * s


def _pre_body(h, tab_ref, gattn_ref, w1_ref, gq_ref, wq_ref, gkv_ref, wkn_ref, wvt_ref,
              qat_ref, ka_ref, vat_ref, qbt_ref, kb_ref, vbt_ref):
    def tab(i):
        return tab_ref[:, i * LANES:(i + 1) * LANES]

    u = (_rms(h) * gattn_ref[...]).astype(BF16)
    proj = jnp.dot(u, w1_ref[...], preferred_element_type=F32)

    caq, saq = tab(T_CAQ), tab(T_SAQ)
    for i in range(SWA_Q_W // LANES):
        sl = slice(C_QA + i * LANES, C_QA + (i + 1) * LANES)
        qat_ref[0, 0, i * LANES:(i + 1) * LANES, :] = _rope(proj[:, sl], caq, saq).T.astype(BF16)
    ka_ref[0] = _rope(proj[:, C_KA:C_KA + LANES], tab(T_CA), tab(T_SA)).astype(BF16)
    vat = proj[:, C_VA:C_VA + LANES].T
    vat_ref[0, 0] = vat[:, :HALF].astype(BF16)
    vat_ref[0, 1] = vat[:, HALF:].astype(BF16)

    qln = (_rms(proj[:, C_QL:C_QL + MLA_Q_RANK]) * gq_ref[...]).astype(BF16)
    qb = jnp.dot(qln, wq_ref[...], preferred_element_type=F32)
    cbq, sbq = tab(T_CBQ), tab(T_SBQ)
    for hd in range(MLA_HEADS):
        sl = slice(hd * LANES, (hd + 1) * LANES)
        qbt_ref[0, 0, sl, :] = _rope(qb[:, sl], cbq, sbq).T.astype(BF16)

    kvn = _rms(proj[:, C_KVL:C_KVL + MLA_KV_RANK]) * gkv_ref[...]
    kn = jnp.dot(kvn.astype(BF16), wkn_ref[...], preferred_element_type=F32)
    kr = _rope(proj[:, C_KR:C_KR + LANES], tab(T_CB), tab(T_SB))
    for hd in range(MLA_HEADS):
        sl = slice(hd * LANES, (hd + 1) * LANES)
        kb_ref[0, :, sl] = (kn[:, sl] + kr).astype(BF16)
    kvn_t = kvn.T.astype(BF16)
    vbt_ref[0, 0] = jnp.dot(wvt_ref[...], kvn_t, preferred_element_type=F32).astype(BF16)


def _run_pipelined(tasks, depth):
    pending = []
    for score_fn, finish_fn in tasks:
        pending.append((finish_fn, score_fn()))
        if len(pending) > depth:
            fin, s = pending.pop(0)
            fin(s)
    for fin, s in pending:
        fin(s)


def _make_attn_kernel(front):
    def attn_kernel(sinks_ref, qat_ref, kap_ref, kac_ref, vap_ref, vac_ref, qbt_ref, kb_ref,
                    vbt_ref, o_ref, m_s, acc_s, s_buf):
        j = pl.program_id(1)

        def with_ones(v):
            return jnp.concatenate([v, jnp.ones((ONES_ROWS, v.shape[1]), v.dtype)], axis=0)

        kwin = jnp.concatenate([kap_ref[0], kac_ref[0]], axis=0)
        lane = lax.broadcasted_iota(jnp.int32, (1, LANES), 1)
        first_kv = (lane % (LANES // 2)) < (SWA_HEAD_DIM // 2)
        zero = jnp.zeros_like(kwin)
        kg = (jnp.where(first_kv, kwin, zero), jnp.where(first_kv, zero, kwin))
        vwin = jnp.concatenate([vap_ref[0, 0], vac_ref[0, 0], vac_ref[0, 1]], axis=1)
        vg_aug = [with_ones(vwin[g * SWA_HEAD_DIM:(g + 1) * SWA_HEAD_DIM, :])
                  for g in range(SWA_KV_HEADS)]
        rr = lax.broadcasted_iota(jnp.int32, (BLK, BLK), 0)
        cc = lax.broadcasted_iota(jnp.int32, (BLK, BLK), 1)
        d = (cc % HALF) + HALF - rr
        band = (d >= 0) & (d < WINDOW)
        swa_valid = [band & (rr + (j * BLK - HALF + hq * HALF) >= front) for hq in range(2)]
        first_head_cols = lax.broadcasted_iota(jnp.int32, (1, BLK), 1) < HALF

        def swa_task(pair, hq, g):
            hd0, hd1 = pair[0] + SWA_GROUP * g, pair[1] + SWA_GROUP * g
            keys = slice(hq * HALF, hq * HALF + BLK)
            cols = slice(hq * HALF, (hq + 1) * HALF)

            def score():
                w = jnp.concatenate([qat_ref[0, 0, i * LANES:(i + 1) * LANES, cols] for i in pair],
                                    axis=1)
                return jnp.dot(kg[g][keys], w, preferred_element_type=F32)

            def finish(s):
                s = jnp.where(swa_valid[hq], s, NEG)
                sink = jnp.where(first_head_cols, sinks_ref[hd0], sinks_ref[hd1]) * LOG2E
                m = jnp.maximum(jnp.max(s, axis=0, keepdims=True), sink)
                p = jnp.exp2(s - m).astype(BF16)
                o = jnp.dot(vg_aug[g][:, keys], p, preferred_element_type=F32)
                l = o[SWA_HEAD_DIM:SWA_HEAD_DIM + 1] + jnp.exp2(sink - m)
                on = (o[:SWA_HEAD_DIM] * (1.0 / l)).astype(o_ref.dtype)
                o_ref[0, 0, hd0 * SWA_HEAD_DIM:(hd0 + 1) * SWA_HEAD_DIM, cols] = on[:, :HALF]
                o_ref[0, 0, hd1 * SWA_HEAD_DIM:(hd1 + 1) * SWA_HEAD_DIM, cols] = on[:, HALF:]

            return score, finish

        def qh(hd):
            return qbt_ref[0, 0, hd * LANES:(hd + 1) * LANES, :]

        def kblk(i, hd):
            return kb_ref[0, pl.ds(pl.multiple_of(i * BLK, BLK), BLK), hd * LANES:(hd + 1) * LANES]

        diag_mask = (rr <= cc) & (rr + j * BLK >= front)

        def v_rows(hd):
            return slice(hd * MLA_V_DIM, (hd + 1) * MLA_V_DIM)

        def acc_rows(hd):
            return slice(hd * ACC_ROWS, (hd + 1) * ACC_ROWS)

        def first_task(hd):
            def score():
                km = kb_ref[0, front:BLK, hd * LANES:(hd + 1) * LANES]
                return (jnp.dot(kblk(j, hd), qh(hd), preferred_element_type=F32),
                        jnp.dot(km, qh(hd), preferred_element_type=F32))

            def finish(ss):
                sd = jnp.where(diag_mask, ss[0], NEG)
                sm = jnp.where(j >= 1, ss[1], NEG)
                m = jnp.maximum(jnp.max(sd, axis=0, keepdims=True), jnp.max(sm, axis=0, keepdims=True))
                pd = jnp.exp2(sd - m).astype(BF16)
                pm = jnp.exp2(sm - m).astype(BF16)
                m_s[hd:hd + 1, :] = m
                vd = with_ones(vbt_ref[0, j, v_rows(hd), :])
                vm = with_ones(vbt_ref[0, 0, v_rows(hd), front:BLK])
                acc_s[acc_rows(hd), :] = (jnp.dot(vd, pd, preferred_element_type=F32)
                                          + jnp.dot(vm, pm, preferred_element_type=F32))

            return score, finish

        def block_score(i, hd, slot):
            s_buf[slot] = jnp.dot(kblk(i, hd), qh(hd), preferred_element_type=F32)

        def block_finish(i, hd, slot):
            s = s_buf[slot]
            m_old = m_s[hd:hd + 1, :]
            m_new = jnp.maximum(m_old, jnp.max(s, axis=0, keepdims=True))
            alpha = jnp.exp2(m_old - m_new)
            p = jnp.exp2(s - m_new).astype(BF16)
            m_s[hd:hd + 1, :] = m_new
            vi = with_ones(vbt_ref[0, i, v_rows(hd), :])
            acc_s[acc_rows(hd), :] = acc_s[acc_rows(hd), :] * alpha + jnp.dot(
                vi, p, preferred_element_type=F32)

        def lookahead_task(hd):
            return (lambda: jnp.dot(kb_ref[0, BLK:2 * BLK, hd * LANES:(hd + 1) * LANES], qh(hd),
                                    preferred_element_type=F32),
                    lambda s: s_buf.__setitem__(hd, s))

        tasks = [lookahead_task(hd) for hd in range(PIPE_DEPTH)]
        tasks += [swa_task(pair, hq, g) for pair in ((0, 1), (2, 3)) for hq in range(2)
                  for g in range(SWA_KV_HEADS)]
        tasks += [first_task(hd) for hd in range(MLA_HEADS)]
        _run_pipelined(tasks, PIPE_DEPTH)

        def run_blocks(first_blk, nb):
            for t in range(nb * MLA_HEADS):
                la = t + PIPE_DEPTH
                block_score(first_blk + la // MLA_HEADS, la % MLA_HEADS, la % S_SLOTS)
                block_finish(first_blk + t // MLA_HEADS, t % MLA_HEADS, t % S_SLOTS)

        n_blocks = jnp.maximum(j - 1, 0)
        n_pairs = lax.shift_right_logical(n_blocks, 1)

        def pair_body(k, carry):
            run_blocks(1 + 2 * k, 2)
            return carry

        def single_body(k, carry):
            run_blocks(1 + 2 * n_pairs, 1)
            return carry

        lax.fori_loop(0, n_pairs, pair_body, 0)
        lax.fori_loop(0, n_blocks & 1, single_body, 0)

        for hd in range(MLA_HEADS):
            acc = acc_s[acc_rows(hd), :]
            o = acc[:MLA_V_DIM] * (1.0 / acc[MLA_V_DIM:MLA_V_DIM + 1])
            o_ref[0, 0, SWA_Q_W + hd * MLA_V_DIM:SWA_Q_W + (hd + 1) * MLA_V_DIM, :] = o.astype(o_ref.dtype)

    return attn_kernel


def _post_body(h, o_ref, gmix_ref, wo_ref, gffn_ref, wg_ref, wu_ref, wd_ref):
    a = o_ref[0, 0].astype(F32)
    aa, bb = a[:SWA_Q_W], a[SWA_Q_W:]
    ia = lax.rsqrt(jnp.mean(aa * aa, axis=0, keepdims=True) + EPS)
    ib = lax.rsqrt(jnp.mean(bb * bb, axis=0, keepdims=True) + EPS)
    y = jnp.concatenate([aa * ia, bb * ib], axis=0)
    mix = (y.T * gmix_ref[...]).astype(BF16)
    h1 = h + jnp.dot(mix, wo_ref[...], preferred_element_type=F32)
    u = (_rms(h1) * gffn_ref[...]).astype(BF16)
    g = jnp.dot(u, wg_ref[...], preferred_element_type=F32)
    up = jnp.dot(u, wu_ref[...], preferred_element_type=F32)
    act = (g * jax.nn.sigmoid(g) * up).astype(BF16)
    return h1 + jnp.dot(act, wd_ref[...], preferred_element_type=F32)


N_POST_W = 6
N_PRE_W = 8
N_PRE_OUT = 6


def _first_kernel(front, x_ref, meta_ref, *refs):
    pre_w, h_out, pre_out = refs[:N_PRE_W], refs[N_PRE_W], refs[N_PRE_W + 1:]
    meta_blk = jnp.concatenate([jnp.zeros((front, meta_ref.shape[1]), F32), meta_ref[...]], axis=0)
    h = jnp.where(pl.program_id(0) == 0, meta_blk, x_ref[0])
    h_out[0] = h
    _pre_body(h, *pre_w, *pre_out)


def _mid_kernel(h_ref, o_ref, *refs):
    post_w, pre_w = refs[:N_POST_W], refs[N_POST_W:N_POST_W + N_PRE_W]
    h_out, pre_out = refs[N_POST_W + N_PRE_W], refs[N_POST_W + N_PRE_W + 1:]
    h = _post_body(h_ref[0], o_ref, *post_w)
    h_out[0] = h
    _pre_body(h, *pre_w, *pre_out)


def _last_kernel(h_ref, o_ref, *refs):
    post_w, gfinal_ref, out_ref = refs[:N_POST_W], refs[N_POST_W], refs[N_POST_W + 1]
    h = _post_body(h_ref[0], o_ref, *post_w)
    out_ref[0] = _rms(h) * gfinal_ref[...]


def _gather_cols(w, idx):
    idx = np.asarray(idx)
    cols = jnp.take(w, jnp.asarray(np.maximum(idx, 0)), axis=-1)
    return jnp.where(jnp.asarray(idx >= 0), cols, 0.0)


def _w1_index(o1, o2, o3, o4, o5):
    half = SWA_HEAD_DIM // 2
    idx = []
    for i in range(SWA_GROUP):
        pair = (i, i + SWA_GROUP)
        for part in range(2):
            for hd in pair:
                idx += [hd * SWA_HEAD_DIM + part * half + t for t in range(half)]
    for part in range(2):
        for g in range(SWA_KV_HEADS):
            idx += [o1 + g * SWA_HEAD_DIM + part * half + t for t in range(half)]
    idx += list(range(o2, o5))
    idx += _mla_group_index(o5, None)
    return idx


def _mla_group_index(rope_base, nope_base):
    hn, hr = MLA_NOPE_DIM // 2, MLA_ROPE_DIM // 2
    pad = LANES // 2 - hn - hr
    idx = []
    for part in range(2):
        idx += [-1] * hn if nope_base is None else [nope_base + part * hn + t for t in range(hn)]
        idx += [-1] * hr if rope_base is None else [rope_base + part * hr + t for t in range(hr)]
        idx += [-1] * pad
    return idx


def _rope_tables(tp, front):
    pos = (jnp.arange(tp) - front).astype(F32)

    def cs(d):
        inv = ROPE_THETA ** (-jnp.arange(0, d, 2, dtype=F32) / d)
        ang = pos[:, None] * inv[None, :]
        return jnp.cos(ang), jnp.sin(ang)

    ca, sa = cs(SWA_HEAD_DIM)
    CA = jnp.concatenate([ca] * 4, axis=1)
    SA = jnp.concatenate([-sa, -sa, sa, sa], axis=1)
    cb, sb = cs(MLA_ROPE_DIM)
    hn, hr = MLA_NOPE_DIM // 2, MLA_ROPE_DIM // 2
    pad = LANES // 2 - hn - hr
    one_n, one_p = jnp.ones((tp, hn), F32), jnp.ones((tp, pad), F32)
    z_n, z_p = jnp.zeros((tp, hn), F32), jnp.zeros((tp, pad), F32)
    CB = jnp.concatenate([one_n, cb, one_p, one_n, cb, one_p], axis=1)
    SB = jnp.concatenate([z_n, -sb, z_p, z_n, sb, z_p], axis=1)
    sa_scale = SWA_HEAD_DIM ** -0.5 * LOG2E
    sb_scale = MLA_QK_DIM ** -0.5 * LOG2E
    return jnp.concatenate([CA * sa_scale, SA * sa_scale, CA, SA,
                            CB * sb_scale, SB * sb_scale, CB, SB], axis=1)


def _const_spec(shape):
    return pl.BlockSpec(shape, lambda *_: (0,) * len(shape), pipeline_mode=pl.Buffered(1))


def kernel(x, meta_tokens, attn_norm, w_in, q_norm, w_q_up, kv_norm, w_kv_up, sinks,
           out_norm_swa, out_norm_mla, w_o, ffn_norm, w_gate, w_up, w_down, final_norm):
    B, S, D = x.shape
    depth = w_in.shape[0]
    d_ff = w_gate.shape[-1]
    assert S % BLK == 0 and N_META <= BLK and WINDOW <= HALF and MLA_HEADS % S_SLOTS == 0
    front = BLK - N_META
    tp = BLK + S
    nblk = tp // BLK

    o1 = SWA_Q_W
    o2 = o1 + SWA_KV_W
    o3 = o2 + SWA_KV_W
    o4 = o3 + MLA_Q_RANK
    o5 = o4 + MLA_KV_RANK

    w1 = _gather_cols(w_in, _w1_index(o1, o2, o3, o4, o5)).astype(BF16)
    wq_idx, wkn_idx, wv_idx = [], [], []
    for hd in range(MLA_HEADS):
        wq_idx += _mla_group_index(hd * MLA_QK_DIM + MLA_NOPE_DIM, hd * MLA_QK_DIM)
        wkn_idx += _mla_group_index(None, hd * (MLA_NOPE_DIM + MLA_V_DIM))
        wv_idx += [hd * (MLA_NOPE_DIM + MLA_V_DIM) + MLA_NOPE_DIM + t for t in range(MLA_V_DIM)]
    wq = _gather_cols(w_q_up, wq_idx).astype(BF16)
    wkn = _gather_cols(w_kv_up, wkn_idx).astype(BF16)
    wvt = jnp.swapaxes(_gather_cols(w_kv_up, wv_idx), 1, 2).astype(BF16)
    wo = w_o.astype(BF16)
    wg = w_gate.astype(BF16)
    wu = w_up.astype(BF16)
    wd = w_down.astype(BF16)
    gmix = jnp.concatenate([out_norm_swa, out_norm_mla], axis=-1)
    tabs = _rope_tables(tp, front)

    hw = MLA_HEADS * LANES
    h_spec = pl.BlockSpec((1, BLK, D), lambda j, b: (b, j, 0))
    o_spec = pl.BlockSpec((1, 1, MIX_W, BLK), lambda j, b: (b, j, 0, 0))
    pre_w_specs = [
        pl.BlockSpec((BLK, 8 * LANES), lambda j, b: (j, 0)),
        _const_spec((1, D)),
        _const_spec((D, W1_COLS)),
        _const_spec((1, MLA_Q_RANK)),
        _const_spec((MLA_Q_RANK, hw)),
        _const_spec((1, MLA_KV_RANK)),
        _const_spec((MLA_KV_RANK, hw)),
        _const_spec((MLA_OUT_W, MLA_KV_RANK)),
    ]
    post_w_specs = [
        _const_spec((1, MIX_W)),
        _const_spec((MIX_W, D)),
        _const_spec((1, D)),
        _const_spec((D, d_ff)),
        _const_spec((D, d_ff)),
        _const_spec((d_ff, D)),
    ]
    pre_out_specs = [
        pl.BlockSpec((1, 1, SWA_Q_W, BLK), lambda j, b: (b, j, 0, 0)),
        pl.BlockSpec((1, BLK, LANES), lambda j, b: (b, j, 0)),
        pl.BlockSpec((1, 2, LANES, HALF), lambda j, b: (b, j, 0, 0)),
        pl.BlockSpec((1, 1, hw, BLK), lambda j, b: (b, j, 0, 0)),
        pl.BlockSpec((1, BLK, hw), lambda j, b: (b, j, 0)),
        pl.BlockSpec((1, 1, MLA_OUT_W, BLK), lambda j, b: (b, j, 0, 0)),
    ]
    pre_out_shapes = [
        jax.ShapeDtypeStruct((B, nblk, SWA_Q_W, BLK), BF16),
        jax.ShapeDtypeStruct((B, tp, LANES), BF16),
        jax.ShapeDtypeStruct((B, 2 * nblk, LANES, HALF), BF16),
        jax.ShapeDtypeStruct((B, nblk, hw, BLK), BF16),
        jax.ShapeDtypeStruct((B, tp, hw), BF16),
        jax.ShapeDtypeStruct((B, nblk, MLA_OUT_W, BLK), BF16),
    ]
    h_shape = jax.ShapeDtypeStruct((B, tp, D), F32)
    seq_params = dict(dimension_semantics=("arbitrary", "arbitrary"))

    def pre_w(l):
        return (tabs, attn_norm[l][None], w1[l], q_norm[l][None], wq[l], kv_norm[l][None],
                wkn[l], wvt[l])

    def post_w(l):
        return (gmix[l][None], wo[l], ffn_norm[l][None], wg[l], wu[l], wd[l])

    first_call = pl.pallas_call(
        functools.partial(_first_kernel, front),
        grid=(nblk, B),
        in_specs=[pl.BlockSpec((1, BLK, D), lambda j, b: (b, jnp.maximum(j - 1, 0), 0)),
                  _const_spec((N_META, D))] + pre_w_specs,
        out_specs=[h_spec] + pre_out_specs,
        out_shape=[h_shape] + pre_out_shapes,
        compiler_params=pltpu.CompilerParams(vmem_limit_bytes=40 * MIB, **seq_params),
        name="first",
    )

    attn_call = pl.pallas_call(
        _make_attn_kernel(front),
        grid=(B, nblk),
        in_specs=[
            pl.BlockSpec(memory_space=pltpu.SMEM),
            pl.BlockSpec((1, 1, SWA_Q_W, BLK), lambda b, j: (b, j, 0, 0)),
            pl.BlockSpec((1, HALF, LANES), lambda b, j: (b, jnp.maximum(2 * j - 1, 0), 0)),
            pl.BlockSpec((1, BLK, LANES), lambda b, j: (b, j, 0)),
            pl.BlockSpec((1, 1, LANES, HALF), lambda b, j: (b, jnp.maximum(2 * j - 1, 0), 0, 0)),
            pl.BlockSpec((1, 2, LANES, HALF), lambda b, j: (b, j, 0, 0)),
            pl.BlockSpec((1, 1, hw, BLK), lambda b, j: (b, j, 0, 0)),
            pl.BlockSpec((1, tp, hw), lambda b, j: (b, 0, 0)),
            pl.BlockSpec((1, nblk, MLA_OUT_W, BLK), lambda b, j: (b, 0, 0, 0)),
        ],
        out_specs=pl.BlockSpec((1, 1, MIX_W, BLK), lambda b, j: (b, j, 0, 0)),
        out_shape=jax.ShapeDtypeStruct((B, nblk, MIX_W, BLK), BF16),
        scratch_shapes=[
            pltpu.VMEM((MLA_HEADS, BLK), F32),
            pltpu.VMEM((MLA_HEADS * ACC_ROWS, BLK), F32),
            pltpu.VMEM((S_SLOTS, BLK, BLK), F32),
        ],
        compiler_params=pltpu.CompilerParams(vmem_limit_bytes=48 * MIB, **seq_params),
        name="attn",
    )

    mid_call = pl.pallas_call(
        _mid_kernel,
        grid=(nblk, B),
        in_specs=[h_spec, o_spec] + post_w_specs + pre_w_specs,
        out_specs=[h_spec] + pre_out_specs,
        out_shape=[h_shape] + pre_out_shapes,
        compiler_params=pltpu.CompilerParams(vmem_limit_bytes=60 * MIB, **seq_params),
        name="mid",
    )

    last_call = pl.pallas_call(
        _last_kernel,
        grid=(B, S // BLK),
        in_specs=[pl.BlockSpec((1, BLK, D), lambda b, j: (b, j + 1, 0)),
                  pl.BlockSpec((1, 1, MIX_W, BLK), lambda b, j: (b, j + 1, 0, 0))]
                 + post_w_specs + [_const_spec((1, D))],
        out_specs=pl.BlockSpec((1, BLK, D), lambda b, j: (b, j, 0)),
        out_shape=jax.ShapeDtypeStruct((B, S, D), x.dtype),
        compiler_params=pltpu.CompilerParams(vmem_limit_bytes=56 * MIB, **seq_params),
        name="last",
    )

    h, *pre_out = first_call(x, meta_tokens.astype(F32), *pre_w(0))
    for l in range(depth):
        qat, ka, vat, qbt, kb, vbt = pre_out
        o_t = attn_call(sinks[l], qat, ka, ka, vat, vat, qbt, kb, vbt)
        if l + 1 < depth:
            h, *pre_out = mid_call(h, o_t, *post_w(l), *pre_w(l + 1))
    return last_call(h, o_t, *post_w(depth - 1), final_norm[None])
```

```python
import functools

import numpy as np
import jax
import jax.numpy as jnp
from jax import lax
from jax.experimental import pallas as pl
from jax.experimental.pallas import tpu as pltpu

N_META = 16
WINDOW = 128
ROPE_THETA = 10000.0
EPS = 1e-6
NEG = -1e30
SWA_HEADS = 8
SWA_KV_HEADS = 2
SWA_HEAD_DIM = 64
SWA_GROUP = SWA_HEADS // SWA_KV_HEADS
MLA_HEADS = 8
MLA_Q_RANK = 256
MLA_KV_RANK = 128
MLA_NOPE_DIM = 64
MLA_ROPE_DIM = 32
MLA_V_DIM = 64
MLA_QK_DIM = MLA_NOPE_DIM + MLA_ROPE_DIM
SWA_Q_W = SWA_HEADS * SWA_HEAD_DIM
SWA_KV_W = SWA_KV_HEADS * SWA_HEAD_DIM
MLA_OUT_W = MLA_HEADS * MLA_V_DIM
MIX_W = SWA_Q_W + MLA_OUT_W

LANES = 128
BLK = 256
HALF = BLK // 2
MIB = 1024 * 1024
PIPE_DEPTH = 3
S_SLOTS = PIPE_DEPTH + 1
ONES_ROWS = 16
ACC_ROWS = MLA_V_DIM + ONES_ROWS
LOG2E = 1.4426950408889634
ROWS = 2

C_QA = 0
C_KA = C_QA + SWA_Q_W
C_VA = C_KA + SWA_KV_W
C_QL = C_VA + SWA_KV_W
C_KVL = C_QL + MLA_Q_RANK
C_KR = C_KVL + MLA_KV_RANK
W1_COLS = C_KR + LANES

T_CAQ, T_SAQ, T_CA, T_SA, T_CBQ, T_SBQ, T_CB, T_SB = range(8)

F32 = jnp.float32
BF16 = jnp.bfloat16


def _rms(x):
    return x * lax.rsqrt(jnp.mean(x * x, axis=-1, keepdims=True) + EPS)


def _nt(a, b):
    return lax.dot_general(a, b, (((1,), (1,)), ((), ())), preferred_element_type=F32)


def _rope(x, c, s):
    return x * c + pltpu.roll(x, LANES // 2, 1) * s


def _pre_body(r, h, tab_ref, gattn_ref, w1_ref, gq_ref, wq_ref, gkv_ref, wkn_ref, wvt_ref,
              qa_ref, ka_ref, vat_ref, qb_ref, kb_ref, vbt_ref):
    def tab(i):
        return tab_ref[:, i * LANES:(i + 1) * LANES]

    u = (_rms(h) * gattn_ref[...]).astype(BF16)
    proj = jnp.dot(u, w1_ref[...], preferred_element_type=F32)

    caq, saq = tab(T_CAQ), tab(T_SAQ)
    for i in range(SWA_Q_W // LANES):
        sl = slice(C_QA + i * LANES, C_QA + (i + 1) * LANES)
        qa_ref[r, :, i * LANES:(i + 1) * LANES] = _rope(proj[:, sl], caq, saq).astype(BF16)
    ka_ref[r] = _rope(proj[:, C_KA:C_KA + LANES], tab(T_CA), tab(T_SA)).astype(BF16)
    vat = proj[:, C_VA:C_VA + LANES].T
    vat_ref[r, 0] = vat[:, :HALF].astype(BF16)
    vat_ref[r, 1] = vat[:, HALF:].astype(BF16)

    qln = (_rms(proj[:, C_QL:C_QL + MLA_Q_RANK]) * gq_ref[...]).astype(BF16)
    qb = jnp.dot(qln, wq_ref[...], preferred_element_type=F32)
    cbq, sbq = tab(T_CBQ), tab(T_SBQ)
    for hd in range(MLA_HEADS):
        sl = slice(hd * LANES, (hd + 1) * LANES)
        qb_ref[r, :, sl] = _rope(qb[:, sl], cbq, sbq).astype(BF16)

    kvn = _rms(proj[:, C_KVL:C_KVL + MLA_KV_RANK]) * gkv_ref[...]
    kn = jnp.dot(kvn.astype(BF16), wkn_ref[...], preferred_element_type=F32)
    kr = _rope(proj[:, C_KR:C_KR + LANES], tab(T_CB), tab(T_SB))
    for hd in range(MLA_HEADS):
        sl = slice(hd * LANES, (hd + 1) * LANES)
        kb_ref[r, :, sl] = (kn[:, sl] + kr).astype(BF16)
    kvn_t = kvn.T.astype(BF16)
    vbt_ref[r, 0] = jnp.dot(wvt_ref[...], kvn_t, preferred_element_type=F32).astype(BF16)


def _run_pipelined(tasks, depth):
    pending = []
    for score_fn, finish_fn in tasks:
        pending.append((finish_fn, score_fn()))
        if len(pending) > depth:
            fin, s = pending.pop(0)
            fin(s)
    for fin, s in pending:
        fin(s)


def _make_attn_kernel(front):
    def attn_kernel(sinks_ref, qa_ref, kap_ref, kac_ref, vap_ref, vac_ref, qb_ref, kb_ref,
                    vbt_ref, o_ref, m_s, acc_s, s_buf):
        j = pl.program_id(1)

        def with_ones(v):
            return jnp.concatenate([v, jnp.ones((ONES_ROWS, v.shape[1]), v.dtype)], axis=0)

        kwin = jnp.concatenate([kap_ref[0], kac_ref[0]], axis=0)
        lane = lax.broadcasted_iota(jnp.int32, (1, LANES), 1)
        first_kv = (lane % (LANES // 2)) < (SWA_HEAD_DIM // 2)
        zero = jnp.zeros_like(kwin)
        kg = (jnp.where(first_kv, kwin, zero), jnp.where(first_kv, zero, kwin))
        vwin = jnp.concatenate([vap_ref[0, 0], vac_ref[0, 0], vac_ref[0, 1]], axis=1)
        vg_aug = [with_ones(vwin[g * SWA_HEAD_DIM:(g + 1) * SWA_HEAD_DIM, :])
                  for g in range(SWA_KV_HEADS)]
        rr = lax.broadcasted_iota(jnp.int32, (BLK, BLK), 0)
        cc = lax.broadcasted_iota(jnp.int32, (BLK, BLK), 1)
        d = (cc % HALF) + HALF - rr
        band = (d >= 0) & (d < WINDOW)
        swa_valid = [band & (rr + (j * BLK - HALF + hq * HALF) >= front) for hq in range(2)]
        first_head_cols = lax.broadcasted_iota(jnp.int32, (1, BLK), 1) < HALF

        def swa_task(pair, hq, g):
            hd0, hd1 = pair[0] + SWA_GROUP * g, pair[1] + SWA_GROUP * g
            keys = slice(hq * HALF, hq * HALF + BLK)
            cols = slice(hq * HALF, (hq + 1) * HALF)

            def score():
                w = jnp.concatenate([qa_ref[0, cols, i * LANES:(i + 1) * LANES] for i in pair],
                                    axis=0)
                return _nt(kg[g][keys], w)

            def finish(s):
                s = jnp.where(swa_valid[hq], s, NEG)
                sink = jnp.where(first_head_cols, sinks_ref[hd0], sinks_ref[hd1]) * LOG2E
                m = jnp.maximum(jnp.max(s, axis=0, keepdims=True), sink)
                p = jnp.exp2(s - m).astype(BF16)
                o = jnp.dot(vg_aug[g][:, keys], p, preferred_element_type=F32)
                l = o[SWA_HEAD_DIM:SWA_HEAD_DIM + 1] + jnp.exp2(sink - m)
                on = (o[:SWA_HEAD_DIM] * (1.0 / l)).astype(o_ref.dtype)
                o_ref[0, 0, hd0 * SWA_HEAD_DIM:(hd0 + 1) * SWA_HEAD_DIM, cols] = on[:, :HALF]
                o_ref[0, 0, hd1 * SWA_HEAD_DIM:(hd1 + 1) * SWA_HEAD_DIM, cols] = on[:, HALF:]

            return score, finish

        def qh(hd):
            return qb_ref[0, :, hd * LANES:(hd + 1) * LANES]

        def kblk(i, hd):
            return kb_ref[0, pl.ds(pl.multiple_of(i * BLK, BLK), BLK), hd * LANES:(hd + 1) * LANES]

        diag_mask = (rr <= cc) & (rr + j * BLK >= front)

        def v_rows(hd):
            return slice(hd * MLA_V_DIM, (hd + 1) * MLA_V_DIM)

        def acc_rows(hd):
            return slice(hd * ACC_ROWS, (hd + 1) * ACC_ROWS)

        def first_task(hd):
            def score():
                km = kb_ref[0, front:BLK, hd * LANES:(hd + 1) * LANES]
                return _nt(kblk(j, hd), qh(hd)), _nt(km, qh(hd))

            def finish(ss):
                sd = jnp.where(diag_mask, ss[0], NEG)
                sm = jnp.where(j >= 1, ss[1], NEG)
                m = jnp.maximum(jnp.max(sd, axis=0, keepdims=True), jnp.max(sm, axis=0, keepdims=True))
                pd = jnp.exp2(sd - m).astype(BF16)
                pm = jnp.exp2(sm - m).astype(BF16)
                m_s[hd:hd + 1, :] = m
                vd = with_ones(vbt_ref[0, j, v_rows(hd), :])
                vm = with_ones(vbt_ref[0, 0, v_rows(hd), front:BLK])
                acc_s[acc_rows(hd), :] = (jnp.dot(vd, pd, preferred_element_type=F32)
                                          + jnp.dot(vm, pm, preferred_element_type=F32))

            return score, finish

        def block_score(i, hd, slot):
            s_buf[slot] = _nt(kblk(i, hd), qh(hd))

        def block_finish(i, hd, slot):
            s = s_buf[slot]
            m_old = m_s[hd:hd + 1, :]
            m_new = jnp.maximum(m_old, jnp.max(s, axis=0, keepdims=True))
            alpha = jnp.exp2(m_old - m_new)
            p = jnp.exp2(s - m_new).astype(BF16)
            m_s[hd:hd + 1, :] = m_new
            vi = with_ones(vbt_ref[0, i, v_rows(hd), :])
            acc_s[acc_rows(hd), :] = acc_s[acc_rows(hd), :] * alpha + jnp.dot(
                vi, p, preferred_element_type=F32)

        def lookahead_task(hd):
            return (lambda: _nt(kb_ref[0, BLK:2 * BLK, hd * LANES:(hd + 1) * LANES], qh(hd)),
                    lambda s: s_buf.__setitem__(hd, s))

        tasks = [lookahead_task(hd) for hd in range(PIPE_DEPTH)]
        tasks += [swa_task(pair, hq, g) for pair in ((0, 1), (2, 3)) for hq in range(2)
                  for g in range(SWA_KV_HEADS)]
        tasks += [first_task(hd) for hd in range(MLA_HEADS)]
        _run_pipelined(tasks, PIPE_DEPTH)

        def run_blocks(first_blk, nb):
            for t in range(nb * MLA_HEADS):
                la = t + PIPE_DEPTH
                block_score(first_blk + la // MLA_HEADS, la % MLA_HEADS, la % S_SLOTS)
                block_finish(first_blk + t // MLA_HEADS, t % MLA_HEADS, t % S_SLOTS)

        n_blocks = jnp.maximum(j - 1, 0)
        n_pairs = lax.shift_right_logical(n_blocks, 1)

        def pair_body(k, carry):
            run_blocks(1 + 2 * k, 2)
            return carry

        def single_body(k, carry):
            run_blocks(1 + 2 * n_pairs, 1)
            return carry

        lax.fori_loop(0, n_pairs, pair_body, 0)
        lax.fori_loop(0, n_blocks & 1, single_body, 0)

        for hd in range(MLA_HEADS):
            acc = acc_s[acc_rows(hd), :]
            o = acc[:MLA_V_DIM] * (1.0 / acc[MLA_V_DIM:MLA_V_DIM + 1])
            o_ref[0, 0, SWA_Q_W + hd * MLA_V_DIM:SWA_Q_W + (hd + 1) * MLA_V_DIM, :] = o.astype(o_ref.dtype)

    return attn_kernel


def _post_body(r, h, o_ref, gmix_ref, wo_ref, gffn_ref, wg_ref, wu_ref, wd_ref):
    a = o_ref[r, 0].astype(F32)
    aa, bb = a[:SWA_Q_W], a[SWA_Q_W:]
    ia = lax.rsqrt(jnp.mean(aa * aa, axis=0, keepdims=True) + EPS)
    ib = lax.rsqrt(jnp.mean(bb * bb, axis=0, keepdims=True) + EPS)
    y = jnp.concatenate([aa * ia, bb * ib], axis=0)
    mix = (y.T * gmix_ref[...]).astype(BF16)
    h1 = h + jnp.dot(mix, wo_ref[...], preferred_element_type=F32)
    u = (_rms(h1) * gffn_ref[...]).astype(BF16)
    g = jnp.dot(u, wg_ref[...], preferred_element_type=F32)
    up = jnp.dot(u, wu_ref[...], preferred_element_type=F32)
    act = (g * jax.nn.sigmoid(g) * up).astype(BF16)
    return h1 + jnp.dot(act, wd_ref[...], preferred_element_type=F32)


N_POST_W = 6
N_PRE_W = 8


def _first_kernel(front, x_ref, meta_ref, *refs):
    pre_w, h_out, pre_out = refs[:N_PRE_W], refs[N_PRE_W], refs[N_PRE_W + 1:]
    meta_blk = jnp.concatenate([jnp.zeros((front, meta_ref.shape[1]), F32), meta_ref[...]], axis=0)
    for r in range(ROWS):
        h = jnp.where(pl.program_id(0) == 0, meta_blk, x_ref[r])
        h_out[r] = h
        _pre_body(r, h, *pre_w, *pre_out)


def _mid_kernel(h_ref, o_ref, *refs):
    post_w, pre_w = refs[:N_POST_W], refs[N_POST_W:N_POST_W + N_PRE_W]
    h_out, pre_out = refs[N_POST_W + N_PRE_W], refs[N_POST_W + N_PRE_W + 1:]
    hs = [_post_body(r, h_ref[r], o_ref, *post_w) for r in range(ROWS)]
    for r in range(ROWS):
        h_out[r] = hs[r]
        _pre_body(r, hs[r], *pre_w, *pre_out)


def _last_kernel(h_ref, o_ref, *refs):
    post_w, gfinal_ref, out_ref = refs[:N_POST_W], refs[N_POST_W], refs[N_POST_W + 1]
    for r in range(ROWS):
        h = _post_body(r, h_ref[r], o_ref, *post_w)
        out_ref[r] = _rms(h) * gfinal_ref[...]


def _gather_cols(w, idx):
    idx = np.asarray(idx)
    cols = jnp.take(w, jnp.asarray(np.maximum(idx, 0)), axis=-1)
    return jnp.where(jnp.asarray(idx >= 0), cols, 0.0)


def _w1_index(o1, o2, o3, o4, o5):
    half = SWA_HEAD_DIM // 2
    idx = []
    for i in range(SWA_GROUP):
        pair = (i, i + SWA_GROUP)
        for part in range(2):
            for hd in pair:
                idx += [hd * SWA_HEAD_DIM + part * half + t for t in range(half)]
    for part in range(2):
        for g in range(SWA_KV_HEADS):
            idx += [o1 + g * SWA_HEAD_DIM + part * half + t for t in range(half)]
    idx += list(range(o2, o5))
    idx += _mla_group_index(o5, None)
    return idx


def _mla_group_index(rope_base, nope_base):
    hn, hr = MLA_NOPE_DIM // 2, MLA_ROPE_DIM // 2
    pad = LANES // 2 - hn - hr
    idx = []
    for part in range(2):
        idx += [-1] * hn if nope_base is None else [nope_base + part * hn + t for t in range(hn)]
        idx += [-1] * hr if rope_base is None else [rope_base + part * hr + t for t in range(hr)]
        idx += [-1] * pad
    return idx


def _rope_tables(tp, front):
    pos = (jnp.arange(tp) - front).astype(F32)

    def cs(d):
        inv = ROPE_THETA ** (-jnp.arange(0, d, 2, dtype=F32) / d)
        ang = pos[:, None] * inv[None, :]
        return jnp.cos(ang), jnp.sin(ang)

    ca, sa = cs(SWA_HEAD_DIM)
    CA = jnp.concatenate([ca] * 4, axis=1)
    SA = jnp.concatenate([-sa, -sa, sa, sa], axis=1)
    cb, sb = cs(MLA_ROPE_DIM)
    hn, hr = MLA_NOPE_DIM // 2, MLA_ROPE_DIM // 2
    pad = LANES // 2 - hn - hr
    one_n, one_p = jnp.ones((tp, hn), F32), jnp.ones((tp, pad), F32)
    z_n, z_p = jnp.zeros((tp, hn), F32), jnp.zeros((tp, pad), F32)
    CB = jnp.concatenate([one_n, cb, one_p, one_n, cb, one_p], axis=1)
    SB = jnp.concatenate([z_n, -sb, z_p, z_n, sb, z_p], axis=1)
    sa_scale = SWA_HEAD_DIM ** -0.5 * LOG2E
    sb_scale = MLA_QK_DIM ** -0.5 * LOG2E
    return jnp.concatenate([CA * sa_scale, SA * sa_scale, CA, SA,
                            CB * sb_scale, SB * sb_scale, CB, SB], axis=1)


def _const_spec(shape):
    return pl.BlockSpec(shape, lambda *_: (0,) * len(shape), pipeline_mode=pl.Buffered(1))


def kernel(x, meta_tokens, attn_norm, w_in, q_norm, w_q_up, kv_norm, w_kv_up, sinks,
           out_norm_swa, out_norm_mla, w_o, ffn_norm, w_gate, w_up, w_down, final_norm):
    B, S, D = x.shape
    depth = w_in.shape[0]
    d_ff = w_gate.shape[-1]
    assert S % BLK == 0 and N_META <= BLK and WINDOW <= HALF and MLA_HEADS % S_SLOTS == 0 and B % ROWS == 0
    front = BLK - N_META
    tp = BLK + S
    nblk = tp // BLK

    o1 = SWA_Q_W
    o2 = o1 + SWA_KV_W
    o3 = o2 + SWA_KV_W
    o4 = o3 + MLA_Q_RANK
    o5 = o4 + MLA_KV_RANK

    w1 = _gather_cols(w_in, _w1_index(o1, o2, o3, o4, o5)).astype(BF16)
    wq_idx, wkn_idx, wv_idx = [], [], []
    for hd in range(MLA_HEADS):
        wq_idx += _mla_group_index(hd * MLA_QK_DIM + MLA_NOPE_DIM, hd * MLA_QK_DIM)
        wkn_idx += _mla_group_index(None, hd * (MLA_NOPE_DIM + MLA_V_DIM))
        wv_idx += [hd * (MLA_NOPE_DIM + MLA_V_DIM) + MLA_NOPE_DIM + t for t in range(MLA_V_DIM)]
    wq = _gather_cols(w_q_up, wq_idx).astype(BF16)
    wkn = _gather_cols(w_kv_up, wkn_idx).astype(BF16)
    wvt = jnp.swapaxes(_gather_cols(w_kv_up, wv_idx), 1, 2).astype(BF16)
    wo = w_o.astype(BF16)
    wg = w_gate.astype(BF16)
    wu = w_up.astype(BF16)
    wd = w_down.astype(BF16)
    gmix = jnp.concatenate([out_norm_swa, out_norm_mla], axis=-1)
    tabs = _rope_tables(tp, front)

    hw = MLA_HEADS * LANES
    h_spec = pl.BlockSpec((ROWS, BLK, D), lambda j, b: (b, j, 0))
    o_spec = pl.BlockSpec((ROWS, 1, MIX_W, BLK), lambda j, b: (b, j, 0, 0))
    pre_w_specs = [
        pl.BlockSpec((BLK, 8 * LANES), lambda j, b: (j, 0)),
        _const_spec((1, D)),
        _const_spec((D, W1_COLS)),
        _const_spec((1, MLA_Q_RANK)),
        _const_spec((MLA_Q_RANK, hw)),
        _const_spec((1, MLA_KV_RANK)),
        _const_spec((MLA_KV_RANK, hw)),
        _const_spec((MLA_OUT_W, MLA_KV_RANK)),
    ]
    post_w_specs = [
        _const_spec((1, MIX_W)),
        _const_spec((MIX_W, D)),
        _const_spec((1, D)),
        _const_spec((D, d_ff)),
        _const_spec((D, d_ff)),
        _const_spec((d_ff, D)),
    ]
    pre_out_specs = [
        pl.BlockSpec((ROWS, BLK, SWA_Q_W), lambda j, b: (b, j, 0)),
        pl.BlockSpec((ROWS, BLK, LANES), lambda j, b: (b, j, 0)),
        pl.BlockSpec((ROWS, 2, LANES, HALF), lambda j, b: (b, j, 0, 0)),
        pl.BlockSpec((ROWS, BLK, hw), lambda j, b: (b, j, 0)),
        pl.BlockSpec((ROWS, BLK, hw), lambda j, b: (b, j, 0)),
        pl.BlockSpec((ROWS, 1, MLA_OUT_W, BLK), lambda j, b: (b, j, 0, 0)),
    ]
    pre_out_shapes = [
        jax.ShapeDtypeStruct((B, tp, SWA_Q_W), BF16),
        jax.ShapeDtypeStruct((B, tp, LANES), BF16),
        jax.ShapeDtypeStruct((B, 2 * nblk, LANES, HALF), BF16),
        jax.ShapeDtypeStruct((B, tp, hw), BF16),
        jax.ShapeDtypeStruct((B, tp, hw), BF16),
        jax.ShapeDtypeStruct((B, nblk, MLA_OUT_W, BLK), BF16),
    ]
    h_shape = jax.ShapeDtypeStruct((B, tp, D), F32)
    seq_params = dict(dimension_semantics=("arbitrary", "arbitrary"))

    def pre_w(l):
        return (tabs, attn_norm[l][None], w1[l], q_norm[l][None], wq[l], kv_norm[l][None],
                wkn[l], wvt[l])

    def post_w(l):
        return (gmix[l][None], wo[l], ffn_norm[l][None], wg[l], wu[l], wd[l])

    first_call = pl.pallas_call(
        functools.partial(_first_kernel, front),
        grid=(nblk, B // ROWS),
        in_specs=[pl.BlockSpec((ROWS, BLK, D), lambda j, b: (b, jnp.maximum(j - 1, 0), 0)),
                  _const_spec((N_META, D))] + pre_w_specs,
        out_specs=[h_spec] + pre_out_specs,
        out_shape=[h_shape] + pre_out_shapes,
        compiler_params=pltpu.CompilerParams(vmem_limit_bytes=40 * MIB, **seq_params),
        name="first",
    )

    attn_call = pl.pallas_call(
        _make_attn_kernel(front),
        grid=(B, nblk),
        in_specs=[
            pl.BlockSpec(memory_space=pltpu.SMEM),
            pl.BlockSpec((1, BLK, SWA_Q_W), lambda b, j: (b, j, 0)),
            pl.BlockSpec((1, HALF, LANES), lambda b, j: (b, jnp.maximum(2 * j - 1, 0), 0)),
            pl.BlockSpec((1, BLK, LANES), lambda b, j: (b, j, 0)),
            pl.BlockSpec((1, 1, LANES, HALF), lambda b, j: (b, jnp.maximum(2 * j - 1, 0), 0, 0)),
            pl.BlockSpec((1, 2, LANES, HALF), lambda b, j: (b, j, 0, 0)),
            pl.BlockSpec((1, BLK, hw), lambda b, j: (b, j, 0)),
            pl.BlockSpec((1, tp, hw), lambda b, j: (b, 0, 0)),
            pl.BlockSpec((1, nblk, MLA_OUT_W, BLK), lambda b, j: (b, 0, 0, 0)),
        ],
        out_specs=pl.BlockSpec((1, 1, MIX_W, BLK), lambda b, j: (b, j, 0, 0)),
        out_shape=jax.ShapeDtypeStruct((B, nblk, MIX_W, BLK), BF16),
        scratch_shapes=[
            pltpu.VMEM((MLA_HEADS, BLK), F32),
            pltpu.VMEM((MLA_HEADS * ACC_ROWS, BLK), F32),
            pltpu.VMEM((S_SLOTS, BLK, BLK), F32),
        ],
        compiler_params=pltpu.CompilerParams(vmem_limit_bytes=48 * MIB, **seq_params),
        name="attn",
    )

    mid_call = pl.pallas_call(
        _mid_kernel,
        grid=(nblk, B // ROWS),
        in_specs=[h_spec, o_spec] + post_w_specs + pre_w_specs,
        out_specs=[h_spec] + pre_out_specs,
        out_shape=[h_shape] + pre_out_shapes,
        compiler_params=pltpu.CompilerParams(vmem_limit_bytes=60 * MIB, **seq_params),
        name="mid",
    )

    last_call = pl.pallas_call(
        _last_kernel,
        grid=(B // ROWS, S // BLK),
        in_specs=[pl.BlockSpec((ROWS, BLK, D), lambda b, j: (b, j + 1, 0)),
                  pl.BlockSpec((ROWS, 1, MIX_W, BLK), lambda b, j: (b, j + 1, 0, 0))]
                 + post_w_specs + [_const_spec((1, D))],
        out_specs=pl.BlockSpec((ROWS, BLK, D), lambda b, j: (b, j, 0)),
        out_shape=jax.ShapeDtypeStruct((B, S, D), x.dtype),
        compiler_params=pltpu.CompilerParams(vmem_limit_bytes=56 * MIB, **seq_params),
        name="last",
    )

    h, *pre_out = first_call(x, meta_tokens.astype(F32), *pre_w(0))
    for l in range(depth):
        qa, ka, vat, qb, kb, vbt = pre_out
        o_t = attn_call(sinks[l], qa, ka, ka, vat, vat, qb, kb, vbt)
        if l + 1 < depth:
            h, *pre_out = mid_call(h, o_t, *post_w(l), *pre_w(l + 1))
    return last_call(h, o_t, *post_w(depth - 1), final_norm[None])
```

```python
import functools

import jax
import jax.numpy as jnp
from jax import lax
from jax.experimental import pallas as pl
from jax.experimental.pallas import tpu as pltpu

N_META = 16
WINDOW = 128
ROPE_THETA = 10000.0
EPS = 1e-6
NEG = -1e30
SWA_HEADS = 8
SWA_KV_HEADS = 2
SWA_HEAD_DIM = 64
SWA_GROUP = SWA_HEADS // SWA_KV_HEADS
MLA_HEADS = 8
MLA_Q_RANK = 256
MLA_KV_RANK = 128
MLA_NOPE_DIM = 64
MLA_ROPE_DIM = 32
MLA_V_DIM = 64
MLA_QK_DIM = MLA_NOPE_DIM + MLA_ROPE_DIM
SWA_Q_W = SWA_HEADS * SWA_HEAD_DIM
SWA_KV_W = SWA_KV_HEADS * SWA_HEAD_DIM
MLA_OUT_W = MLA_HEADS * MLA_V_DIM
MIX_W = SWA_Q_W + MLA_OUT_W

LANES = 128
BLK = 256
HALF = BLK // 2
MIB = 1024 * 1024
PIPE_DEPTH = 7
S_SLOTS = PIPE_DEPTH + 1
ONES_ROWS = 16
ACC_ROWS = MLA_V_DIM + ONES_ROWS
LOG2E = 1.4426950408889634
ROWS = 2

C_QA = 0
C_KA = C_QA + SWA_Q_W
C_VA = C_KA + SWA_KV_W
C_QL = C_VA + SWA_KV_W
C_KVL = C_QL + MLA_Q_RANK
C_KR = C_KVL + MLA_KV_RANK
W1_COLS = C_KR + LANES

T_CAQ, T_SAQ, T_CA, T_SA, T_CBQ, T_SBQ, T_CB, T_SB = range(8)

F32 = jnp.float32
BF16 = jnp.bfloat16


def _rms(x):
    return x * lax.rsqrt(jnp.mean(x * x, axis=-1, keepdims=True) + EPS)


def _nt(a, b):
    return lax.dot_general(a, b, (((1,), (1,)), ((), ())), preferred_element_type=F32)


def _rope(x, c, s):
    return x * c + pltpu.roll(x, LANES // 2, 1) * s


def _pre_body(r, h, tab_ref, gattn_ref, w1_ref, gq_ref, wq_ref, gkv_ref, wkn_ref, wvt_ref,
              q_ref, ka_ref, vat_ref, kb_ref, vbt_ref):
    def tab(i):
        return tab_ref[:, i * LANES:(i + 1) * LANES]

    u = (_rms(h) * gattn_ref[...]).astype(BF16)
    proj = jnp.dot(u, w1_ref[...], preferred_element_type=F32)

    caq, saq = tab(T_CAQ), tab(T_SAQ)
    for i in range(SWA_Q_W // LANES):
        sl = slice(C_QA + i * LANES, C_QA + (i + 1) * LANES)
        q_ref[r, :, i * LANES:(i + 1) * LANES] = _rope(proj[:, sl], caq, saq).astype(BF16)
    ka_ref[r] = _rope(proj[:, C_KA:C_KA + LANES], tab(T_CA), tab(T_SA)).astype(BF16)
    vat = proj[:, C_VA:C_VA + LANES].T
    vat_ref[r, 0] = vat[:, :HALF].astype(BF16)
    vat_ref[r, 1] = vat[:, HALF:].astype(BF16)

    qln = (_rms(proj[:, C_QL:C_QL + MLA_Q_RANK]) * gq_ref[...]).astype(BF16)
    qb = jnp.dot(qln, wq_ref[...], preferred_element_type=F32)
    cbq, sbq = tab(T_CBQ), tab(T_SBQ)
    for hd in range(MLA_HEADS):
        sl = slice(hd * LANES, (hd + 1) * LANES)
        q_ref[r, :, SWA_Q_W + hd * LANES:SWA_Q_W + (hd + 1) * LANES] = _rope(
            qb[:, sl], cbq, sbq).astype(BF16)

    kvn = _rms(proj[:, C_KVL:C_KVL + MLA_KV_RANK]) * gkv_ref[...]
    kn = jnp.dot(kvn.astype(BF16), wkn_ref[...], preferred_element_type=F32)
    kr = _rope(proj[:, C_KR:C_KR + LANES], tab(T_CB), tab(T_SB))
    for hd in range(MLA_HEADS):
        sl = slice(hd * LANES, (hd + 1) * LANES)
        kb_ref[r, :, sl] = (kn[:, sl] + kr).astype(BF16)
    kvn_t = kvn.T.astype(BF16)
    vbt_ref[r, 0] = jnp.dot(wvt_ref[...], kvn_t, preferred_element_type=F32).astype(BF16)


def _run_pipelined(tasks, depth):
    pending = []
    for score_fn, finish_fn in tasks:
        pending.append((finish_fn, score_fn()))
        if len(pending) > depth:
            fin, s = pending.pop(0)
            fin(s)
    for fin, s in pending:
        fin(s)


def _make_attn_kernel(front):
    def attn_kernel(sinks_ref, q_ref, ka_ref, vat_ref, kb_ref, vbt_ref, o_ref, m_s, acc_s, s_buf):
        j = pl.program_id(1)

        def with_ones(v):
            return jnp.concatenate([v, jnp.ones((ONES_ROWS, v.shape[1]), v.dtype)], axis=0)

        prev_half = jnp.maximum(2 * j - 1, 0)
        kwin = jnp.concatenate(
            [ka_ref[0, pl.ds(pl.multiple_of(prev_half * HALF, HALF), HALF), :],
             ka_ref[0, pl.ds(pl.multiple_of(j * BLK, BLK), BLK), :]], axis=0)
        lane = lax.broadcasted_iota(jnp.int32, (1, LANES), 1)
        first_kv = (lane % (LANES // 2)) < (SWA_HEAD_DIM // 2)
        zero = jnp.zeros_like(kwin)
        kg = (jnp.where(first_kv, kwin, zero), jnp.where(first_kv, zero, kwin))
        vwin = jnp.concatenate([vat_ref[0, prev_half], vat_ref[0, 2 * j], vat_ref[0, 2 * j + 1]],
                               axis=1)
        vg_aug = [with_ones(vwin[g * SWA_HEAD_DIM:(g + 1) * SWA_HEAD_DIM, :])
                  for g in range(SWA_KV_HEADS)]
        rr = lax.broadcasted_iota(jnp.int32, (BLK, BLK), 0)
        cc = lax.broadcasted_iota(jnp.int32, (BLK, BLK), 1)
        d = (cc % HALF) + HALF - rr
        band = (d >= 0) & (d < WINDOW)
        swa_valid = [band & (rr + (j * BLK - HALF + hq * HALF) >= front) for hq in range(2)]
        first_head_cols = lax.broadcasted_iota(jnp.int32, (1, BLK), 1) < HALF

        def swa_task(pair, hq, g):
            hd0, hd1 = pair[0] + SWA_GROUP * g, pair[1] + SWA_GROUP * g
            keys = slice(hq * HALF, hq * HALF + BLK)
            cols = slice(hq * HALF, (hq + 1) * HALF)

            def score():
                w = jnp.concatenate([q_ref[0, cols, i * LANES:(i + 1) * LANES] for i in pair],
                                    axis=0)
                return _nt(kg[g][keys], w)

            def finish(s):
                s = jnp.where(swa_valid[hq], s, NEG)
                sink = jnp.where(first_head_cols, sinks_ref[hd0], sinks_ref[hd1]) * LOG2E
                m = jnp.maximum(jnp.max(s, axis=0, keepdims=True), sink)
                p = jnp.exp2(s - m).astype(BF16)
                o = jnp.dot(vg_aug[g][:, keys], p, preferred_element_type=F32)
                l = o[SWA_HEAD_DIM:SWA_HEAD_DIM + 1] + jnp.exp2(sink - m)
                on = (o[:SWA_HEAD_DIM] * (1.0 / l)).astype(o_ref.dtype)
                o_ref[0, 0, hd0 * SWA_HEAD_DIM:(hd0 + 1) * SWA_HEAD_DIM, cols] = on[:, :HALF]
                o_ref[0, 0, hd1 * SWA_HEAD_DIM:(hd1 + 1) * SWA_HEAD_DIM, cols] = on[:, HALF:]

            return score, finish

        def qh(hd):
            return q_ref[0, :, SWA_Q_W + hd * LANES:SWA_Q_W + (hd + 1) * LANES]

        def kblk(i, hd):
            return kb_ref[0, pl.ds(pl.multiple_of(i * BLK, BLK), BLK), hd * LANES:(hd + 1) * LANES]

        diag_mask = (rr <= cc) & (rr + j * BLK >= front)

        def v_rows(hd):
            return slice(hd * MLA_V_DIM, (hd + 1) * MLA_V_DIM)

        def acc_rows(hd):
            return slice(hd * ACC_ROWS, (hd + 1) * ACC_ROWS)

        def first_task(hd):
            def score():
                km = kb_ref[0, front:BLK, hd * LANES:(hd + 1) * LANES]
                return _nt(kblk(j, hd), qh(hd)), _nt(km, qh(hd))

            def finish(ss):
                sd = jnp.where(diag_mask, ss[0], NEG)
                sm = jnp.where(j >= 1, ss[1], NEG)
                m = jnp.maximum(jnp.max(sd, axis=0, keepdims=True), jnp.max(sm, axis=0, keepdims=True))
                pd = jnp.exp2(sd - m).astype(BF16)
                pm = jnp.exp2(sm - m).astype(BF16)
                m_s[hd:hd + 1, :] = m
                vd = with_ones(vbt_ref[0, j, v_rows(hd), :])
                vm = with_ones(vbt_ref[0, 0, v_rows(hd), front:BLK])
                acc_s[acc_rows(hd), :] = (jnp.dot(vd, pd, preferred_element_type=F32)
                                          + jnp.dot(vm, pm, preferred_element_type=F32))

            return score, finish

        def block_score(i, hd, slot):
            s_buf[slot] = _nt(kblk(i, hd), qh(hd))

        def block_finish(i, hd, slot):
            s = s_buf[slot]
            m_old = m_s[hd:hd + 1, :]
            m_new = jnp.maximum(m_old, jnp.max(s, axis=0, keepdims=True))
            alpha = jnp.exp2(m_old - m_new)
            p = jnp.exp2(s - m_new).astype(BF16)
            m_s[hd:hd + 1, :] = m_new
            vi = with_ones(vbt_ref[0, i, v_rows(hd), :])
            acc_s[acc_rows(hd), :] = acc_s[acc_rows(hd), :] * alpha + jnp.dot(
                vi, p, preferred_element_type=F32)

        def lookahead_task(hd):
            return (lambda: _nt(kb_ref[0, BLK:2 * BLK, hd * LANES:(hd + 1) * LANES], qh(hd)),
                    lambda s: s_buf.__setitem__(hd, s))

        tasks = [lookahead_task(hd) for hd in range(PIPE_DEPTH)]
        tasks += [swa_task(pair, hq, g) for pair in ((0, 1), (2, 3)) for hq in range(2)
                  for g in range(SWA_KV_HEADS)]
        tasks += [first_task(hd) for hd in range(MLA_HEADS)]
        _run_pipelined(tasks, PIPE_DEPTH)

        def run_blocks(first_blk, nb):
            for t in range(nb * MLA_HEADS):
                la = t + PIPE_DEPTH
                block_score(first_blk + la // MLA_HEADS, la % MLA_HEADS, la % S_SLOTS)
                block_finish(first_blk + t // MLA_HEADS, t % MLA_HEADS, t % S_SLOTS)

        n_blocks = jnp.maximum(j - 1, 0)
        n_pairs = lax.shift_right_logical(n_blocks, 1)

        def pair_body(k, carry):
            run_blocks(1 + 2 * k, 2)
            return carry

        def single_body(k, carry):
            run_blocks(1 + 2 * n_pairs, 1)
            return carry

        lax.fori_loop(0, n_pairs, pair_body, 0)
        lax.fori_loop(0, n_blocks & 1, single_body, 0)

        for hd in range(MLA_HEADS):
            acc = acc_s[acc_rows(hd), :]
            o = acc[:MLA_V_DIM] * (1.0 / acc[MLA_V_DIM:MLA_V_DIM + 1])
            o_ref[0, 0, SWA_Q_W + hd * MLA_V_DIM:SWA_Q_W + (hd + 1) * MLA_V_DIM, :] = o.astype(o_ref.dtype)

    return attn_kernel


def _post_body(r, h, o_ref, gmix_ref, wo_ref, gffn_ref, wg_ref, wu_ref, wd_ref):
    a = o_ref[r, 0].astype(F32)
    aa, bb = a[:SWA_Q_W], a[SWA_Q_W:]
    ia = lax.rsqrt(jnp.mean(aa * aa, axis=0, keepdims=True) + EPS)
    ib = lax.rsqrt(jnp.mean(bb * bb, axis=0, keepdims=True) + EPS)
    y = jnp.concatenate([aa * ia, bb * ib], axis=0)
    mix = (y.T * gmix_ref[...]).astype(BF16)
    h1 = h + jnp.dot(mix, wo_ref[...], preferred_element_type=F32)
    u = (_rms(h1) * gffn_ref[...]).astype(BF16)
    g = jnp.dot(u, wg_ref[...], preferred_element_type=F32)
    up = jnp.dot(u, wu_ref[...], preferred_element_type=F32)
    act = (g * jax.nn.sigmoid(g) * up).astype(BF16)
    return h1 + jnp.dot(act, wd_ref[...], preferred_element_type=F32)


N_POST_W = 6
N_PRE_W = 8


def _first_kernel(front, x_ref, meta_ref, *refs):
    pre_w, h_out, pre_out = refs[:N_PRE_W], refs[N_PRE_W], refs[N_PRE_W + 1:]
    meta_blk = jnp.concatenate([jnp.zeros((front, meta_ref.shape[1]), F32), meta_ref[...]], axis=0)
    for r in range(ROWS):
        h = jnp.where(pl.program_id(0) == 0, meta_blk, x_ref[r])
        h_out[r] = h
        _pre_body(r, h, *pre_w, *pre_out)


def _mid_kernel(h_ref, o_ref, *refs):
    post_w, pre_w = refs[:N_POST_W], refs[N_POST_W:N_POST_W + N_PRE_W]
    h_out, pre_out = refs[N_POST_W + N_PRE_W], refs[N_POST_W + N_PRE_W + 1:]
    hs = [_post_body(r, h_ref[r], o_ref, *post_w) for r in range(ROWS)]
    for r in range(ROWS):
        h_out[r] = hs[r]
        _pre_body(r, hs[r], *pre_w, *pre_out)


def _last_kernel(h_ref, o_ref, *refs):
    post_w, gfinal_ref, out_ref = refs[:N_POST_W], refs[N_POST_W], refs[N_POST_W + 1]
    for r in range(ROWS):
        h = _post_body(r, h_ref[r], o_ref, *post_w)
        out_ref[r] = _rms(h) * gfinal_ref[...]


def _gather_cols(w, idx):
    idx = list(idx)
    parts, i = [], 0
    while i < len(idx):
        k = i + 1
        if idx[i] < 0:
            while k < len(idx) and idx[k] < 0:
                k += 1
            parts.append(jnp.zeros(w.shape[:-1] + (k - i,), w.dtype))
        else:
            while k < len(idx) and idx[k] == idx[k - 1] + 1:
                k += 1
            parts.append(w[..., idx[i]:idx[i] + (k - i)])
        i = k
    return jnp.concatenate(parts, axis=-1)


def _w1_index(o1, o2, o3, o4, o5):
    half = SWA_HEAD_DIM // 2
    idx = []
    for i in range(SWA_GROUP):
        pair = (i, i + SWA_GROUP)
        for part in range(2):
            for hd in pair:
                idx += [hd * SWA_HEAD_DIM + part * half + t for t in range(half)]
    for part in range(2):
        for g in range(SWA_KV_HEADS):
            idx += [o1 + g * SWA_HEAD_DIM + part * half + t for t in range(half)]
    idx += list(range(o2, o5))
    idx += _mla_group_index(o5, None)
    return idx


def _mla_group_index(rope_base, nope_base):
    hn, hr = MLA_NOPE_DIM // 2, MLA_ROPE_DIM // 2
    pad = LANES // 2 - hn - hr
    idx = []
    for part in range(2):
        idx += [-1] * hn if nope_base is None else [nope_base + part * hn + t for t in range(hn)]
        idx += [-1] * hr if rope_base is None else [rope_base + part * hr + t for t in range(hr)]
        idx += [-1] * pad
    return idx


def _rope_tables(tp, front):
    pos = (jnp.arange(tp) - front).astype(F32)

    def cs(d):
        inv = ROPE_THETA ** (-jnp.arange(0, d, 2, dtype=F32) / d)
        ang = pos[:, None] * inv[None, :]
        return jnp.cos(ang), jnp.sin(ang)

    ca, sa = cs(SWA_HEAD_DIM)
    CA = jnp.concatenate([ca] * 4, axis=1)
    SA = jnp.concatenate([-sa, -sa, sa, sa], axis=1)
    cb, sb = cs(MLA_ROPE_DIM)
    hn, hr = MLA_NOPE_DIM // 2, MLA_ROPE_DIM // 2
    pad = LANES // 2 - hn - hr
    one_n, one_p = jnp.ones((tp, hn), F32), jnp.ones((tp, pad), F32)
    z_n, z_p = jnp.zeros((tp, hn), F32), jnp.zeros((tp, pad), F32)
    CB = jnp.concatenate([one_n, cb, one_p, one_n, cb, one_p], axis=1)
    SB = jnp.concatenate([z_n, -sb, z_p, z_n, sb, z_p], axis=1)
    sa_scale = SWA_HEAD_DIM ** -0.5 * LOG2E
    sb_scale = MLA_QK_DIM ** -0.5 * LOG2E
    return jnp.concatenate([CA * sa_scale, SA * sa_scale, CA, SA,
                            CB * sb_scale, SB * sb_scale, CB, SB], axis=1)


def _const_spec(shape):
    return pl.BlockSpec(shape, lambda *_: (0,) * len(shape), pipeline_mode=pl.Buffered(1))


def kernel(x, meta_tokens, attn_norm, w_in, q_norm, w_q_up, kv_norm, w_kv_up, sinks,
           out_norm_swa, out_norm_mla, w_o, ffn_norm, w_gate, w_up, w_down, final_norm):
    B, S, D = x.shape
    depth = w_in.shape[0]
    d_ff = w_gate.shape[-1]
    assert S % BLK == 0 and N_META <= BLK and WINDOW <= HALF and MLA_HEADS % S_SLOTS == 0 and B % ROWS == 0
    front = BLK - N_META
    tp = BLK + S
    nblk = tp // BLK

    o1 = SWA_Q_W
    o2 = o1 + SWA_KV_W
    o3 = o2 + SWA_KV_W
    o4 = o3 + MLA_Q_RANK
    o5 = o4 + MLA_KV_RANK

    w1 = _gather_cols(w_in, _w1_index(o1, o2, o3, o4, o5)).astype(BF16)
    wq_idx, wkn_idx, wv_idx = [], [], []
    for hd in range(MLA_HEADS):
        wq_idx += _mla_group_index(hd * MLA_QK_DIM + MLA_NOPE_DIM, hd * MLA_QK_DIM)
        wkn_idx += _mla_group_index(None, hd * (MLA_NOPE_DIM + MLA_V_DIM))
        wv_idx += [hd * (MLA_NOPE_DIM + MLA_V_DIM) + MLA_NOPE_DIM + t for t in range(MLA_V_DIM)]
    wq = _gather_cols(w_q_up, wq_idx).astype(BF16)
    wkn = _gather_cols(w_kv_up, wkn_idx).astype(BF16)
    wvt = jnp.swapaxes(_gather_cols(w_kv_up, wv_idx), 1, 2).astype(BF16)
    wo = w_o.astype(BF16)
    wg = w_gate.astype(BF16)
    wu = w_up.astype(BF16)
    wd = w_down.astype(BF16)
    gmix = jnp.concatenate([out_norm_swa, out_norm_mla], axis=-1)
    tabs = _rope_tables(tp, front)

    hw = MLA_HEADS * LANES
    h_spec = pl.BlockSpec((ROWS, BLK, D), lambda j, b: (b, j, 0))
    o_spec = pl.BlockSpec((ROWS, 1, MIX_W, BLK), lambda j, b: (b, j, 0, 0))
    pre_w_specs = [
        pl.BlockSpec((BLK, 8 * LANES), lambda j, b: (j, 0)),
        _const_spec((1, D)),
        _const_spec((D, W1_COLS)),
        _const_spec((1, MLA_Q_RANK)),
        _const_spec((MLA_Q_RANK, hw)),
        _const_spec((1, MLA_KV_RANK)),
        _const_spec((MLA_KV_RANK, hw)),
        _const_spec((MLA_OUT_W, MLA_KV_RANK)),
    ]
    post_w_specs = [
        _const_spec((1, MIX_W)),
        _const_spec((MIX_W, D)),
        _const_spec((1, D)),
        _const_spec((D, d_ff)),
        _const_spec((D, d_ff)),
        _const_spec((d_ff, D)),
    ]
    pre_out_specs = [
        pl.BlockSpec((ROWS, BLK, SWA_Q_W + hw), lambda j, b: (b, j, 0)),
        pl.BlockSpec((ROWS, BLK, LANES), lambda j, b: (b, j, 0)),
        pl.BlockSpec((ROWS, 2, LANES, HALF), lambda j, b: (b, j, 0, 0)),
        pl.BlockSpec((ROWS, BLK, hw), lambda j, b: (b, j, 0)),
        pl.BlockSpec((ROWS, 1, MLA_OUT_W, BLK), lambda j, b: (b, j, 0, 0)),
    ]
    pre_out_shapes = [
        jax.ShapeDtypeStruct((B, tp, SWA_Q_W + hw), BF16),
        jax.ShapeDtypeStruct((B, tp, LANES), BF16),
        jax.ShapeDtypeStruct((B, 2 * nblk, LANES, HALF), BF16),
        jax.ShapeDtypeStruct((B, tp, hw), BF16),
        jax.ShapeDtypeStruct((B, nblk, MLA_OUT_W, BLK), BF16),
    ]
    h_shape = jax.ShapeDtypeStruct((B, tp, D), F32)
    seq_params = dict(dimension_semantics=("arbitrary", "arbitrary"))

    def pre_w(l):
        return (tabs, attn_norm[l][None], w1[l], q_norm[l][None], wq[l], kv_norm[l][None],
                wkn[l], wvt[l])

    def post_w(l):
        return (gmix[l][None], wo[l], ffn_norm[l][None], wg[l], wu[l], wd[l])

    first_call = pl.pallas_call(
        functools.partial(_first_kernel, front),
        grid=(nblk, B // ROWS),
        in_specs=[pl.BlockSpec((ROWS, BLK, D), lambda j, b: (b, jnp.maximum(j - 1, 0), 0)),
                  _const_spec((N_META, D))] + pre_w_specs,
        out_specs=[h_spec] + pre_out_specs,
        out_shape=[h_shape] + pre_out_shapes,
        compiler_params=pltpu.CompilerParams(vmem_limit_bytes=40 * MIB, **seq_params),
        name="first",
    )

    attn_call = pl.pallas_call(
        _make_attn_kernel(front),
        grid=(B, nblk),
        in_specs=[
            pl.BlockSpec(memory_space=pltpu.SMEM),
            pl.BlockSpec((1, BLK, SWA_Q_W + hw), lambda b, j: (b, j, 0)),
            pl.BlockSpec((1, tp, LANES), lambda b, j: (b, 0, 0)),
            pl.BlockSpec((1, 2 * nblk, LANES, HALF), lambda b, j: (b, 0, 0, 0)),
            pl.BlockSpec((1, tp, hw), lambda b, j: (b, 0, 0)),
            pl.BlockSpec((1, nblk, MLA_OUT_W, BLK), lambda b, j: (b, 0, 0, 0)),
        ],
        out_specs=pl.BlockSpec((1, 1, MIX_W, BLK), lambda b, j: (b, j, 0, 0)),
        out_shape=jax.ShapeDtypeStruct((B, nblk, MIX_W, BLK), BF16),
        scratch_shapes=[
            pltpu.VMEM((MLA_HEADS, BLK), F32),
            pltpu.VMEM((MLA_HEADS * ACC_ROWS, BLK), F32),
            pltpu.VMEM((S_SLOTS, BLK, BLK), F32),
        ],
        compiler_params=pltpu.CompilerParams(vmem_limit_bytes=48 * MIB, **seq_params),
        name="attn",
    )

    mid_call = pl.pallas_call(
        _mid_kernel,
        grid=(nblk, B // ROWS),
        in_specs=[h_spec, o_spec] + post_w_specs + pre_w_specs,
        out_specs=[h_spec] + pre_out_specs,
        out_shape=[h_shape] + pre_out_shapes,
        compiler_params=pltpu.CompilerParams(vmem_limit_bytes=60 * MIB, **seq_params),
        name="mid",
    )

    last_call = pl.pallas_call(
        _last_kernel,
        grid=(B // ROWS, S // BLK),
        in_specs=[pl.BlockSpec((ROWS, BLK, D), lambda b, j: (b, j + 1, 0)),
                  pl.BlockSpec((ROWS, 1, MIX_W, BLK), lambda b, j: (b, j + 1, 0, 0))]
                 + post_w_specs + [_const_spec((1, D))],
        out_specs=pl.BlockSpec((ROWS, BLK, D), lambda b, j: (b, j, 0)),
        out_shape=jax.ShapeDtypeStruct((B, S, D), x.dtype),
        compiler_params=pltpu.CompilerParams(vmem_limit_bytes=56 * MIB, **seq_params),
        name="last",
    )

    h, *pre_out = first_call(x, meta_tokens.astype(F32), *pre_w(0))
    for l in range(depth):
        o_t = attn_call(sinks[l], *pre_out)
        if l + 1 < depth:
            h, *pre_out = mid_call(h, o_t, *post_w(l), *pre_w(l + 1))
    return last_call(h, o_t, *post_w(depth - 1), final_norm[None])
```

```python
import functools

import jax
import jax.numpy as jnp
from jax import lax
from jax.experimental import pallas as pl
from jax.experimental.pallas import tpu as pltpu

N_META = 16
WINDOW = 128
ROPE_THETA = 10000.0
EPS = 1e-6
NEG = -1e30
SWA_HEADS = 8
SWA_KV_HEADS = 2
SWA_HEAD_DIM = 64
SWA_GROUP = SWA_HEADS // SWA_KV_HEADS
MLA_HEADS = 8
MLA_Q_RANK = 256
MLA_KV_RANK = 128
MLA_NOPE_DIM = 64
MLA_ROPE_DIM = 32
MLA_V_DIM = 64
MLA_QK_DIM = MLA_NOPE_DIM + MLA_ROPE_DIM
SWA_Q_W = SWA_HEADS * SWA_HEAD_DIM
SWA_KV_W = SWA_KV_HEADS * SWA_HEAD_DIM
MLA_OUT_W = MLA_HEADS * MLA_V_DIM
MIX_W = SWA_Q_W + MLA_OUT_W

LANES = 128
BLK = 256
HALF = BLK // 2
MIB = 1024 * 1024
PIPE_DEPTH = 7
S_SLOTS = PIPE_DEPTH + 1
ONES_ROWS = 16
ACC_ROWS = MLA_V_DIM + ONES_ROWS
LOG2E = 1.4426950408889634
ROWS = 2

C_QA = 0
C_KA = C_QA + SWA_Q_W
C_VA = C_KA + SWA_KV_W
C_QL = C_VA + SWA_KV_W
C_KVL = C_QL + MLA_Q_RANK
C_KR = C_KVL + MLA_KV_RANK
W1_COLS = C_KR + LANES

T_CAQ, T_SAQ, T_CA, T_SA, T_CBQ, T_SBQ, T_CB, T_SB = range(8)

F32 = jnp.float32
BF16 = jnp.bfloat16


def _rms(x):
    return x * lax.rsqrt(jnp.mean(x * x, axis=-1, keepdims=True) + EPS)


def _nt(a, b):
    return lax.dot_general(a, b, (((1,), (1,)), ((), ())), preferred_element_type=F32)


def _rope(x, c, s):
    return x * c + pltpu.roll(x, LANES // 2, 1) * s


def _pre_body(r, h, tab_ref, gattn_ref, w1_ref, gq_ref, wq_ref, gkv_ref, wkn_ref, wvt_ref,
              q_ref, ka_ref, vat_ref, kb_ref, vbt_ref):
    def tab(i):
        return tab_ref[:, i * LANES:(i + 1) * LANES]

    u = (_rms(h) * gattn_ref[...]).astype(BF16)
    proj = jnp.dot(u, w1_ref[...], preferred_element_type=F32)

    caq, saq = tab(T_CAQ), tab(T_SAQ)
    for i in range(SWA_Q_W // LANES):
        sl = slice(C_QA + i * LANES, C_QA + (i + 1) * LANES)
        q_ref[r, :, i * LANES:(i + 1) * LANES] = _rope(proj[:, sl], caq, saq).astype(BF16)
    ka_ref[r] = _rope(proj[:, C_KA:C_KA + LANES], tab(T_CA), tab(T_SA)).astype(BF16)
    vat = proj[:, C_VA:C_VA + LANES].T
    vat_ref[r, 0] = vat[:, :HALF].astype(BF16)
    vat_ref[r, 1] = vat[:, HALF:].astype(BF16)

    qln = (_rms(proj[:, C_QL:C_QL + MLA_Q_RANK]) * gq_ref[...]).astype(BF16)
    qb = jnp.dot(qln, wq_ref[...], preferred_element_type=F32)
    cbq, sbq = tab(T_CBQ), tab(T_SBQ)
    for hd in range(MLA_HEADS):
        sl = slice(hd * LANES, (hd + 1) * LANES)
        q_ref[r, :, SWA_Q_W + hd * LANES:SWA_Q_W + (hd + 1) * LANES] = _rope(
            qb[:, sl], cbq, sbq).astype(BF16)

    kvn = _rms(proj[:, C_KVL:C_KVL + MLA_KV_RANK]) * gkv_ref[...]
    kn = jnp.dot(kvn.astype(BF16), wkn_ref[...], preferred_element_type=F32)
    kr = _rope(proj[:, C_KR:C_KR + LANES], tab(T_CB), tab(T_SB))
    for hd in range(MLA_HEADS):
        sl = slice(hd * LANES, (hd + 1) * LANES)
        kb_ref[r, :, sl] = (kn[:, sl] + kr).astype(BF16)
    kvn_t = kvn.T.astype(BF16)
    vbt_ref[r, 0] = jnp.dot(wvt_ref[...], kvn_t, preferred_element_type=F32).astype(BF16)


def _run_pipelined(tasks, depth):
    pending = []
    for score_fn, finish_fn in tasks:
        pending.append((finish_fn, score_fn()))
        if len(pending) > depth:
            fin, s = pending.pop(0)
            fin(s)
    for fin, s in pending:
        fin(s)


def _make_attn_kernel(front):
    def attn_kernel(sinks_ref, q_ref, ka_ref, vat_ref, kb_ref, vbt_ref, ka0_ref, vat0_ref, kb0_ref,
                    vbt0_ref, o_ref, m_s, acc_s, s_buf):
        pl.when((pl.program_id(1) > 0) | (pl.program_id(0) < ROWS))(
            functools.partial(attn_body, sinks_ref, q_ref, ka_ref, vat_ref, kb_ref, vbt_ref, ka0_ref,
                              vat0_ref, kb0_ref, vbt0_ref, o_ref, m_s, acc_s, s_buf))

    def attn_body(sinks_ref, q_ref, ka_ref, vat_ref, kb_ref, vbt_ref, ka0_ref, vat0_ref, kb0_ref,
                  vbt0_ref, o_ref, m_s, acc_s, s_buf):
        j = pl.program_id(1)

        def with_ones(v):
            return jnp.concatenate([v, jnp.ones((ONES_ROWS, v.shape[1]), v.dtype)], axis=0)

        prev_half = jnp.maximum(2 * j - 1, 0)
        k_prev = jnp.where(j == 1, ka0_ref[0, HALF:, :],
                           ka_ref[0, pl.ds(pl.multiple_of(prev_half * HALF, HALF), HALF), :])
        v_prev = jnp.where(j == 1, vat0_ref[0, 1], vat_ref[0, prev_half])
        kwin = jnp.concatenate(
            [k_prev, ka_ref[0, pl.ds(pl.multiple_of(j * BLK, BLK), BLK), :]], axis=0)
        lane = lax.broadcasted_iota(jnp.int32, (1, LANES), 1)
        first_kv = (lane % (LANES // 2)) < (SWA_HEAD_DIM // 2)
        zero = jnp.zeros_like(kwin)
        kg = (jnp.where(first_kv, kwin, zero), jnp.where(first_kv, zero, kwin))
        vwin = jnp.concatenate([v_prev, vat_ref[0, 2 * j], vat_ref[0, 2 * j + 1]], axis=1)
        vg_aug = [with_ones(vwin[g * SWA_HEAD_DIM:(g + 1) * SWA_HEAD_DIM, :])
                  for g in range(SWA_KV_HEADS)]
        rr = lax.broadcasted_iota(jnp.int32, (BLK, BLK), 0)
        cc = lax.broadcasted_iota(jnp.int32, (BLK, BLK), 1)
        d = (cc % HALF) + HALF - rr
        band = (d >= 0) & (d < WINDOW)
        swa_valid = [band & (rr + (j * BLK - HALF + hq * HALF) >= front) for hq in range(2)]
        first_head_cols = lax.broadcasted_iota(jnp.int32, (1, BLK), 1) < HALF

        def swa_task(pair, hq, g):
            hd0, hd1 = pair[0] + SWA_GROUP * g, pair[1] + SWA_GROUP * g
            keys = slice(hq * HALF, hq * HALF + BLK)
            cols = slice(hq * HALF, (hq + 1) * HALF)

            def score():
                w = jnp.concatenate([q_ref[0, cols, i * LANES:(i + 1) * LANES] for i in pair],
                                    axis=0)
                return _nt(kg[g][keys], w)

            def finish(s):
                s = jnp.where(swa_valid[hq], s, NEG)
                sink = jnp.where(first_head_cols, sinks_ref[hd0], sinks_ref[hd1]) * LOG2E
                m = jnp.maximum(jnp.max(s, axis=0, keepdims=True), sink)
                p = jnp.exp2(s - m).astype(BF16)
                o = jnp.dot(vg_aug[g][:, keys], p, preferred_element_type=F32)
                l = o[SWA_HEAD_DIM:SWA_HEAD_DIM + 1] + jnp.exp2(sink - m)
                on = (o[:SWA_HEAD_DIM] * (1.0 / l)).astype(o_ref.dtype)
                o_ref[0, 0, hd0 * SWA_HEAD_DIM:(hd0 + 1) * SWA_HEAD_DIM, cols] = on[:, :HALF]
                o_ref[0, 0, hd1 * SWA_HEAD_DIM:(hd1 + 1) * SWA_HEAD_DIM, cols] = on[:, HALF:]

            return score, finish

        def qh(hd):
            return q_ref[0, :, SWA_Q_W + hd * LANES:SWA_Q_W + (hd + 1) * LANES]

        def kblk(i, hd):
            return kb_ref[0, pl.ds(pl.multiple_of(i * BLK, BLK), BLK), hd * LANES:(hd + 1) * LANES]

        diag_mask = (rr <= cc) & (rr + j * BLK >= front)

        def v_rows(hd):
            return slice(hd * MLA_V_DIM, (hd + 1) * MLA_V_DIM)

        def acc_rows(hd):
            return slice(hd * ACC_ROWS, (hd + 1) * ACC_ROWS)

        def first_task(hd):
            def score():
                km = kb0_ref[0, front:BLK, hd * LANES:(hd + 1) * LANES]
                return _nt(kblk(j, hd), qh(hd)), _nt(km, qh(hd))

            def finish(ss):
                sd = jnp.where(diag_mask, ss[0], NEG)
                sm = jnp.where(j >= 1, ss[1], NEG)
                m = jnp.maximum(jnp.max(sd, axis=0, keepdims=True), jnp.max(sm, axis=0, keepdims=True))
                pd = jnp.exp2(sd - m).astype(BF16)
                pm = jnp.exp2(sm - m).astype(BF16)
                m_s[hd:hd + 1, :] = m
                vd = with_ones(vbt_ref[0, j, v_rows(hd), :])
                vm = with_ones(vbt0_ref[0, 0, v_rows(hd), front:BLK])
                acc_s[acc_rows(hd), :] = (jnp.dot(vd, pd, preferred_element_type=F32)
                                          + jnp.dot(vm, pm, preferred_element_type=F32))

            return score, finish

        def block_score(i, hd, slot):
            s_buf[slot] = _nt(kblk(i, hd), qh(hd))

        def block_finish(i, hd, slot):
            s = s_buf[slot]
            m_old = m_s[hd:hd + 1, :]
            m_new = jnp.maximum(m_old, jnp.max(s, axis=0, keepdims=True))
            alpha = jnp.exp2(m_old - m_new)
            p = jnp.exp2(s - m_new).astype(BF16)
            m_s[hd:hd + 1, :] = m_new
            vi = with_ones(vbt_ref[0, i, v_rows(hd), :])
            acc_s[acc_rows(hd), :] = acc_s[acc_rows(hd), :] * alpha + jnp.dot(
                vi, p, preferred_element_type=F32)

        def lookahead_task(hd):
            return (lambda: _nt(kb_ref[0, BLK:2 * BLK, hd * LANES:(hd + 1) * LANES], qh(hd)),
                    lambda s: s_buf.__setitem__(hd, s))

        tasks = [lookahead_task(hd) for hd in range(PIPE_DEPTH)]
        tasks += [swa_task(pair, hq, g) for pair in ((0, 1), (2, 3)) for hq in range(2)
                  for g in range(SWA_KV_HEADS)]
        tasks += [first_task(hd) for hd in range(MLA_HEADS)]
        _run_pipelined(tasks, PIPE_DEPTH)

        def run_blocks(first_blk, nb):
            for t in range(nb * MLA_HEADS):
                la = t + PIPE_DEPTH
                block_score(first_blk + la // MLA_HEADS, la % MLA_HEADS, la % S_SLOTS)
                block_finish(first_blk + t // MLA_HEADS, t % MLA_HEADS, t % S_SLOTS)

        n_blocks = jnp.maximum(j - 1, 0)
        n_pairs = lax.shift_right_logical(n_blocks, 1)

        def pair_body(k, carry):
            run_blocks(1 + 2 * k, 2)
            return carry

        def single_body(k, carry):
            run_blocks(1 + 2 * n_pairs, 1)
            return carry

        lax.fori_loop(0, n_pairs, pair_body, 0)
        lax.fori_loop(0, n_blocks & 1, single_body, 0)

        for hd in range(MLA_HEADS):
            acc = acc_s[acc_rows(hd), :]
            o = acc[:MLA_V_DIM] * (1.0 / acc[MLA_V_DIM:MLA_V_DIM + 1])
            o_ref[0, 0, SWA_Q_W + hd * MLA_V_DIM:SWA_Q_W + (hd + 1) * MLA_V_DIM, :] = o.astype(o_ref.dtype)

    return attn_kernel


def _post_body(r, h, o_ref, gmix_ref, wo_ref, gffn_ref, wg_ref, wu_ref, wd_ref):
    a = o_ref[r, 0].astype(F32)
    aa, bb = a[:SWA_Q_W], a[SWA_Q_W:]
    ia = lax.rsqrt(jnp.mean(aa * aa, axis=0, keepdims=True) + EPS)
    ib = lax.rsqrt(jnp.mean(bb * bb, axis=0, keepdims=True) + EPS)
    y = jnp.concatenate([aa * ia, bb * ib], axis=0)
    mix = (y.T * gmix_ref[...]).astype(BF16)
    h1 = h + jnp.dot(mix, wo_ref[...], preferred_element_type=F32)
    u = (_rms(h1) * gffn_ref[...]).astype(BF16)
    g = jnp.dot(u, wg_ref[...], preferred_element_type=F32)
    up = jnp.dot(u, wu_ref[...], preferred_element_type=F32)
    act = (g * jax.nn.sigmoid(g) * up).astype(BF16)
    return h1 + jnp.dot(act, wd_ref[...], preferred_element_type=F32)


N_POST_W = 6
N_PRE_W = 8


def _block_is_needed():
    return (pl.program_id(0) > 0) | (pl.program_id(1) == 0)


def _first_kernel(front, x_ref, meta_ref, *refs):
    pre_w, h_out, pre_out = refs[:N_PRE_W], refs[N_PRE_W], refs[N_PRE_W + 1:]

    @pl.when(_block_is_needed())
    def _():
        meta_blk = jnp.concatenate([jnp.zeros((front, meta_ref.shape[1]), F32), meta_ref[...]],
                                   axis=0)
        for r in range(ROWS):
            h = jnp.where(pl.program_id(0) == 0, meta_blk, x_ref[r])
            h_out[r] = h
            _pre_body(r, h, *pre_w, *pre_out)


def _mid_kernel(h_ref, o_ref, *refs):
    post_w, pre_w = refs[:N_POST_W], refs[N_POST_W:N_POST_W + N_PRE_W]
    h_out, pre_out = refs[N_POST_W + N_PRE_W], refs[N_POST_W + N_PRE_W + 1:]

    @pl.when(_block_is_needed())
    def _():
        hs = [_post_body(r, h_ref[r], o_ref, *post_w) for r in range(ROWS)]
        for r in range(ROWS):
            h_out[r] = hs[r]
            _pre_body(r, hs[r], *pre_w, *pre_out)


def _last_kernel(h_ref, o_ref, *refs):
    post_w, gfinal_ref, out_ref = refs[:N_POST_W], refs[N_POST_W], refs[N_POST_W + 1]
    for r in range(ROWS):
        h = _post_body(r, h_ref[r], o_ref, *post_w)
        out_ref[r] = _rms(h) * gfinal_ref[...]


def _gather_cols(w, idx):
    idx = list(idx)
    parts, i = [], 0
    while i < len(idx):
        k = i + 1
        if idx[i] < 0:
            while k < len(idx) and idx[k] < 0:
                k += 1
            parts.append(jnp.zeros(w.shape[:-1] + (k - i,), w.dtype))
        else:
            while k < len(idx) and idx[k] == idx[k - 1] + 1:
                k += 1
            parts.append(w[..., idx[i]:idx[i] + (k - i)])
        i = k
    return jnp.concatenate(parts, axis=-1)


def _w1_index(o1, o2, o3, o4, o5):
    half = SWA_HEAD_DIM // 2
    idx = []
    for i in range(SWA_GROUP):
        pair = (i, i + SWA_GROUP)
        for part in range(2):
            for hd in pair:
                idx += [hd * SWA_HEAD_DIM + part * half + t for t in range(half)]
    for part in range(2):
        for g in range(SWA_KV_HEADS):
            idx += [o1 + g * SWA_HEAD_DIM + part * half + t for t in range(half)]
    idx += list(range(o2, o5))
    idx += _mla_group_index(o5, None)
    return idx


def _mla_group_index(rope_base, nope_base):
    hn, hr = MLA_NOPE_DIM // 2, MLA_ROPE_DIM // 2
    pad = LANES // 2 - hn - hr
    idx = []
    for part in range(2):
        idx += [-1] * hn if nope_base is None else [nope_base + part * hn + t for t in range(hn)]
        idx += [-1] * hr if rope_base is None else [rope_base + part * hr + t for t in range(hr)]
        idx += [-1] * pad
    return idx


def _rope_tables(tp, front):
    pos = (jnp.arange(tp) - front).astype(F32)

    def cs(d):
        inv = ROPE_THETA ** (-jnp.arange(0, d, 2, dtype=F32) / d)
        ang = pos[:, None] * inv[None, :]
        return jnp.cos(ang), jnp.sin(ang)

    ca, sa = cs(SWA_HEAD_DIM)
    CA = jnp.concatenate([ca] * 4, axis=1)
    SA = jnp.concatenate([-sa, -sa, sa, sa], axis=1)
    cb, sb = cs(MLA_ROPE_DIM)
    hn, hr = MLA_NOPE_DIM // 2, MLA_ROPE_DIM // 2
    pad = LANES // 2 - hn - hr
    one_n, one_p = jnp.ones((tp, hn), F32), jnp.ones((tp, pad), F32)
    z_n, z_p = jnp.zeros((tp, hn), F32), jnp.zeros((tp, pad), F32)
    CB = jnp.concatenate([one_n, cb, one_p, one_n, cb, one_p], axis=1)
    SB = jnp.concatenate([z_n, -sb, z_p, z_n, sb, z_p], axis=1)
    sa_scale = SWA_HEAD_DIM ** -0.5 * LOG2E
    sb_scale = MLA_QK_DIM ** -0.5 * LOG2E
    return jnp.concatenate([CA * sa_scale, SA * sa_scale, CA, SA,
                            CB * sb_scale, SB * sb_scale, CB, SB], axis=1)


def _const_spec(shape):
    return pl.BlockSpec(shape, lambda *_: (0,) * len(shape), pipeline_mode=pl.Buffered(1))


def kernel(x, meta_tokens, attn_norm, w_in, q_norm, w_q_up, kv_norm, w_kv_up, sinks,
           out_norm_swa, out_norm_mla, w_o, ffn_norm, w_gate, w_up, w_down, final_norm):
    B, S, D = x.shape
    depth = w_in.shape[0]
    d_ff = w_gate.shape[-1]
    assert S % BLK == 0 and N_META <= BLK and WINDOW <= HALF and MLA_HEADS % S_SLOTS == 0 and B % ROWS == 0
    front = BLK - N_META
    tp = BLK + S
    nblk = tp // BLK

    o1 = SWA_Q_W
    o2 = o1 + SWA_KV_W
    o3 = o2 + SWA_KV_W
    o4 = o3 + MLA_Q_RANK
    o5 = o4 + MLA_KV_RANK

    w1 = _gather_cols(w_in, _w1_index(o1, o2, o3, o4, o5)).astype(BF16)
    wq_idx, wkn_idx, wv_idx = [], [], []
    for hd in range(MLA_HEADS):
        wq_idx += _mla_group_index(hd * MLA_QK_DIM + MLA_NOPE_DIM, hd * MLA_QK_DIM)
        wkn_idx += _mla_group_index(None, hd * (MLA_NOPE_DIM + MLA_V_DIM))
        wv_idx += [hd * (MLA_NOPE_DIM + MLA_V_DIM) + MLA_NOPE_DIM + t for t in range(MLA_V_DIM)]
    wq = _gather_cols(w_q_up, wq_idx).astype(BF16)
    wkn = _gather_cols(w_kv_up, wkn_idx).astype(BF16)
    wvt = jnp.swapaxes(_gather_cols(w_kv_up, wv_idx), 1, 2).astype(BF16)
    wo = w_o.astype(BF16)
    wg = w_gate.astype(BF16)
    wu = w_up.astype(BF16)
    wd = w_down.astype(BF16)
    gmix = jnp.concatenate([out_norm_swa, out_norm_mla], axis=-1)
    tabs = _rope_tables(tp, front)

    hw = MLA_HEADS * LANES
    h_spec = pl.BlockSpec((ROWS, BLK, D), lambda j, b: (b, j, 0))
    o_spec = pl.BlockSpec((ROWS, 1, MIX_W, BLK), lambda j, b: (b, j, 0, 0))
    pre_w_specs = [
        pl.BlockSpec((BLK, 8 * LANES), lambda j, b: (j, 0)),
        _const_spec((1, D)),
        _const_spec((D, W1_COLS)),
        _const_spec((1, MLA_Q_RANK)),
        _const_spec((MLA_Q_RANK, hw)),
        _const_spec((1, MLA_KV_RANK)),
        _const_spec((MLA_KV_RANK, hw)),
        _const_spec((MLA_OUT_W, MLA_KV_RANK)),
    ]
    post_w_specs = [
        _const_spec((1, MIX_W)),
        _const_spec((MIX_W, D)),
        _const_spec((1, D)),
        _const_spec((D, d_ff)),
        _const_spec((D, d_ff)),
        _const_spec((d_ff, D)),
    ]
    pre_out_specs = [
        pl.BlockSpec((ROWS, BLK, SWA_Q_W + hw), lambda j, b: (b, j, 0)),
        pl.BlockSpec((ROWS, BLK, LANES), lambda j, b: (b, j, 0)),
        pl.BlockSpec((ROWS, 2, LANES, HALF), lambda j, b: (b, j, 0, 0)),
        pl.BlockSpec((ROWS, BLK, hw), lambda j, b: (b, j, 0)),
        pl.BlockSpec((ROWS, 1, MLA_OUT_W, BLK), lambda j, b: (b, j, 0, 0)),
    ]
    pre_out_shapes = [
        jax.ShapeDtypeStruct((B, tp, SWA_Q_W + hw), BF16),
        jax.ShapeDtypeStruct((B, tp, LANES), BF16),
        jax.ShapeDtypeStruct((B, 2 * nblk, LANES, HALF), BF16),
        jax.ShapeDtypeStruct((B, tp, hw), BF16),
        jax.ShapeDtypeStruct((B, nblk, MLA_OUT_W, BLK), BF16),
    ]
    h_shape = jax.ShapeDtypeStruct((B, tp, D), F32)
    seq_params = dict(dimension_semantics=("arbitrary", "arbitrary"))

    def pre_w(l):
        return (tabs, attn_norm[l][None], w1[l], q_norm[l][None], wq[l], kv_norm[l][None],
                wkn[l], wvt[l])

    def post_w(l):
        return (gmix[l][None], wo[l], ffn_norm[l][None], wg[l], wu[l], wd[l])

    first_call = pl.pallas_call(
        functools.partial(_first_kernel, front),
        grid=(nblk, B // ROWS),
        in_specs=[pl.BlockSpec((ROWS, BLK, D), lambda j, b: (b, jnp.maximum(j - 1, 0), 0)),
                  _const_spec((N_META, D))] + pre_w_specs,
        out_specs=[h_spec] + pre_out_specs,
        out_shape=[h_shape] + pre_out_shapes,
        compiler_params=pltpu.CompilerParams(vmem_limit_bytes=40 * MIB, **seq_params),
        name="first",
    )

    attn_call = pl.pallas_call(
        _make_attn_kernel(front),
        grid=(B, nblk),
        in_specs=[
            pl.BlockSpec(memory_space=pltpu.SMEM),
            pl.BlockSpec((1, BLK, SWA_Q_W + hw), lambda b, j: (b, j, 0)),
            pl.BlockSpec((1, tp, LANES), lambda b, j: (b, 0, 0)),
            pl.BlockSpec((1, 2 * nblk, LANES, HALF), lambda b, j: (b, 0, 0, 0)),
            pl.BlockSpec((1, tp, hw), lambda b, j: (b, 0, 0)),
            pl.BlockSpec((1, nblk, MLA_OUT_W, BLK), lambda b, j: (b, 0, 0, 0)),
            _const_spec((1, BLK, LANES)),
            _const_spec((1, 2, LANES, HALF)),
            _const_spec((1, BLK, hw)),
            _const_spec((1, 1, MLA_OUT_W, BLK)),
        ],
        out_specs=pl.BlockSpec((1, 1, MIX_W, BLK), lambda b, j: (b, j, 0, 0)),
        out_shape=jax.ShapeDtypeStruct((B, nblk, MIX_W, BLK), BF16),
        scratch_shapes=[
            pltpu.VMEM((MLA_HEADS, BLK), F32),
            pltpu.VMEM((MLA_HEADS * ACC_ROWS, BLK), F32),
            pltpu.VMEM((S_SLOTS, BLK, BLK), F32),
        ],
        compiler_params=pltpu.CompilerParams(vmem_limit_bytes=48 * MIB, **seq_params),
        name="attn",
    )

    mid_call = pl.pallas_call(
        _mid_kernel,
        grid=(nblk, B // ROWS),
        in_specs=[h_spec, o_spec] + post_w_specs + pre_w_specs,
        out_specs=[h_spec] + pre_out_specs,
        out_shape=[h_shape] + pre_out_shapes,
        compiler_params=pltpu.CompilerParams(vmem_limit_bytes=60 * MIB, **seq_params),
        name="mid",
    )

    last_call = pl.pallas_call(
        _last_kernel,
        grid=(B // ROWS, S // BLK),
        in_specs=[pl.BlockSpec((ROWS, BLK, D), lambda b, j: (b, j + 1, 0)),
                  pl.BlockSpec((ROWS, 1, MIX_W, BLK), lambda b, j: (b, j + 1, 0, 0))]
                 + post_w_specs + [_const_spec((1, D))],
        out_specs=pl.BlockSpec((ROWS, BLK, D), lambda b, j: (b, j, 0)),
        out_shape=jax.ShapeDtypeStruct((B, S, D), x.dtype),
        compiler_params=pltpu.CompilerParams(vmem_limit_bytes=56 * MIB, **seq_params),
        name="last",
    )

    h, *pre_out = first_call(x, meta_tokens.astype(F32), *pre_w(0))
    for l in range(depth):
        q, ka, vat, kb, vbt = pre_out
        o_t = attn_call(sinks[l], q, ka, vat, kb, vbt, ka, vat, kb, vbt)
        if l + 1 < depth:
            h, *pre_out = mid_call(h, o_t, *post_w(l), *pre_w(l + 1))
    return last_call(h, o_t, *post_w(depth - 1), final_norm[None])
```

```python
import functools

import jax
import jax.numpy as jnp
from jax import lax
from jax.experimental import pallas as pl
from jax.experimental.pallas import tpu as pltpu

N_META = 16
WINDOW = 128
ROPE_THETA = 10000.0
EPS = 1e-6
NEG = -1e30
SWA_HEADS = 8
SWA_KV_HEADS = 2
SWA_HEAD_DIM = 64
SWA_GROUP = SWA_HEADS // SWA_KV_HEADS
MLA_HEADS = 8
MLA_Q_RANK = 256
MLA_KV_RANK = 128
MLA_NOPE_DIM = 64
MLA_ROPE_DIM = 32
MLA_V_DIM = 64
MLA_QK_DIM = MLA_NOPE_DIM + MLA_ROPE_DIM
SWA_Q_W = SWA_HEADS * SWA_HEAD_DIM
SWA_KV_W = SWA_KV_HEADS * SWA_HEAD_DIM
MLA_OUT_W = MLA_HEADS * MLA_V_DIM
MIX_W = SWA_Q_W + MLA_OUT_W

LANES = 128
BLK = 256
HALF = BLK // 2
MIB = 1024 * 1024
PIPE_DEPTH = 7
S_SLOTS = PIPE_DEPTH + 1
ONES_ROWS = 16
ACC_ROWS = MLA_V_DIM + ONES_ROWS
LOG2E = 1.4426950408889634
ROWS = 2

C_QA = 0
C_KA = C_QA + SWA_Q_W
C_VA = C_KA + SWA_KV_W
C_QL = C_VA + SWA_KV_W
C_KVL = C_QL + MLA_Q_RANK
C_KR = C_KVL + MLA_KV_RANK
W1_COLS = C_KR + LANES

T_CA, T_SA, T_CB, T_SB = range(4)
N_TABS = 4
QA_SCALE = SWA_HEAD_DIM ** -0.5 * LOG2E
QB_SCALE = MLA_QK_DIM ** -0.5 * LOG2E

F32 = jnp.float32
BF16 = jnp.bfloat16


def _rms(x):
    return x * lax.rsqrt(jnp.mean(x * x, axis=-1, keepdims=True) + EPS)


def _nt(a, b):
    return lax.dot_general(a, b, (((1,), (1,)), ((), ())), preferred_element_type=F32)


def _rope(x, c, s):
    return x * c + pltpu.roll(x, LANES // 2, 1) * s


def _pre_body(r, h, tab_ref, gattn_ref, w1_ref, gq_ref, wq_ref, gkv_ref, wkn_ref, wvt_ref,
              q_ref, ka_ref, vat_ref, kb_ref, vbt_ref):
    def tab(i):
        return tab_ref[:, i * LANES:(i + 1) * LANES]

    u = (_rms(h) * gattn_ref[...]).astype(BF16)
    proj = jnp.dot(u, w1_ref[...], preferred_element_type=F32)

    caq, saq = tab(T_CA) * QA_SCALE, tab(T_SA) * QA_SCALE
    for i in range(SWA_Q_W // LANES):
        sl = slice(C_QA + i * LANES, C_QA + (i + 1) * LANES)
        q_ref[r, :, i * LANES:(i + 1) * LANES] = _rope(proj[:, sl], caq, saq).astype(BF16)
    ka_ref[r] = _rope(proj[:, C_KA:C_KA + LANES], tab(T_CA), tab(T_SA)).astype(BF16)
    vat = proj[:, C_VA:C_VA + LANES].T
    vat_ref[r, 0] = vat[:, :HALF].astype(BF16)
    vat_ref[r, 1] = vat[:, HALF:].astype(BF16)

    qln = (_rms(proj[:, C_QL:C_QL + MLA_Q_RANK]) * gq_ref[...]).astype(BF16)
    qb = jnp.dot(qln, wq_ref[...], preferred_element_type=F32)
    cbq, sbq = tab(T_CB) * QB_SCALE, tab(T_SB) * QB_SCALE
    for hd in range(MLA_HEADS):
        sl = slice(hd * LANES, (hd + 1) * LANES)
        q_ref[r, :, SWA_Q_W + hd * LANES:SWA_Q_W + (hd + 1) * LANES] = _rope(
            qb[:, sl], cbq, sbq).astype(BF16)

    kvn = _rms(proj[:, C_KVL:C_KVL + MLA_KV_RANK]) * gkv_ref[...]
    kn = jnp.dot(kvn.astype(BF16), wkn_ref[...], preferred_element_type=F32)
    kr = _rope(proj[:, C_KR:C_KR + LANES], tab(T_CB), tab(T_SB))
    for hd in range(MLA_HEADS):
        sl = slice(hd * LANES, (hd + 1) * LANES)
        kb_ref[r, :, sl] = (kn[:, sl] + kr).astype(BF16)
    kvn_t = kvn.T.astype(BF16)
    vbt_ref[r, 0] = jnp.dot(wvt_ref[...], kvn_t, preferred_element_type=F32).astype(BF16)


def _run_pipelined(tasks, depth):
    pending = []
    for score_fn, finish_fn in tasks:
        pending.append((finish_fn, score_fn()))
        if len(pending) > depth:
            fin, s = pending.pop(0)
            fin(s)
    for fin, s in pending:
        fin(s)


def _make_attn_kernel(front):
    def attn_kernel(sinks_ref, q_ref, ka_ref, vat_ref, kb_ref, vbt_ref, ka0_ref, vat0_ref, kb0_ref,
                    vbt0_ref, o_ref, m_s, acc_s, s_buf):
        needed = (pl.program_id(1) > 0) | (pl.program_id(0) < ROWS)
        pl.when(needed)(
            functools.partial(attn_body, sinks_ref, q_ref, ka_ref, vat_ref, kb_ref, vbt_ref, ka0_ref,
                              vat0_ref, kb0_ref, vbt0_ref, o_ref, m_s, acc_s, s_buf))

        @pl.when(jnp.logical_not(needed))
        def _():
            o_ref[...] = jnp.zeros(o_ref.shape, o_ref.dtype)

    def attn_body(sinks_ref, q_ref, ka_ref, vat_ref, kb_ref, vbt_ref, ka0_ref, vat0_ref, kb0_ref,
                  vbt0_ref, o_ref, m_s, acc_s, s_buf):
        j = pl.program_id(1)

        def with_ones(v):
            return jnp.concatenate([v, jnp.ones((ONES_ROWS, v.shape[1]), v.dtype)], axis=0)

        prev_half = jnp.maximum(2 * j - 1, 0)
        k_prev = jnp.where(j == 1, ka0_ref[0, HALF:, :],
                           ka_ref[0, pl.ds(pl.multiple_of(prev_half * HALF, HALF), HALF), :])
        v_prev = jnp.where(j == 1, vat0_ref[0, 1], vat_ref[0, prev_half])
        kwin = jnp.concatenate(
            [k_prev, ka_ref[0, pl.ds(pl.multiple_of(j * BLK, BLK), BLK), :]], axis=0)
        lane = lax.broadcasted_iota(jnp.int32, (1, LANES), 1)
        first_kv = (lane % (LANES // 2)) < (SWA_HEAD_DIM // 2)
        zero = jnp.zeros_like(kwin)
        kg = (jnp.where(first_kv, kwin, zero), jnp.where(first_kv, zero, kwin))
        vwin = jnp.concatenate([v_prev, vat_ref[0, 2 * j], vat_ref[0, 2 * j + 1]], axis=1)
        vg_aug = [with_ones(vwin[g * SWA_HEAD_DIM:(g + 1) * SWA_HEAD_DIM, :])
                  for g in range(SWA_KV_HEADS)]
        rr = lax.broadcasted_iota(jnp.int32, (BLK, BLK), 0)
        cc = lax.broadcasted_iota(jnp.int32, (BLK, BLK), 1)
        d = (cc % HALF) + HALF - rr
        band = (d >= 0) & (d < WINDOW)
        swa_valid = [band & (rr + (j * BLK - HALF + hq * HALF) >= front) for hq in range(2)]
        first_head_cols = lax.broadcasted_iota(jnp.int32, (1, BLK), 1) < HALF

        def swa_task(pair, hq, g):
            hd0, hd1 = pair[0] + SWA_GROUP * g, pair[1] + SWA_GROUP * g
            keys = slice(hq * HALF, hq * HALF + BLK)
            cols = slice(hq * HALF, (hq + 1) * HALF)

            def score():
                w = jnp.concatenate([q_ref[0, cols, i * LANES:(i + 1) * LANES] for i in pair],
                                    axis=0)
                return _nt(kg[g][keys], w)

            def finish(s):
                s = jnp.where(swa_valid[hq], s, NEG)
                sink = jnp.where(first_head_cols, sinks_ref[hd0], sinks_ref[hd1]) * LOG2E
                m = jnp.maximum(jnp.max(s, axis=0, keepdims=True), sink)
                p = jnp.exp2(s - m).astype(BF16)
                o = jnp.dot(vg_aug[g][:, keys], p, preferred_element_type=F32)
                l = o[SWA_HEAD_DIM:SWA_HEAD_DIM + 1] + jnp.exp2(sink - m)
                on = (o[:SWA_HEAD_DIM] * (1.0 / l)).astype(o_ref.dtype)
                o_ref[0, 0, hd0 * SWA_HEAD_DIM:(hd0 + 1) * SWA_HEAD_DIM, cols] = on[:, :HALF]
                o_ref[0, 0, hd1 * SWA_HEAD_DIM:(hd1 + 1) * SWA_HEAD_DIM, cols] = on[:, HALF:]

            return score, finish

        def qh(hd):
            return q_ref[0, :, SWA_Q_W + hd * LANES:SWA_Q_W + (hd + 1) * LANES]

        def kblk(i, hd):
            return kb_ref[0, pl.ds(pl.multiple_of(i * BLK, BLK), BLK), hd * LANES:(hd + 1) * LANES]

        diag_mask = (rr <= cc) & (rr + j * BLK >= front)

        def v_rows(hd):
            return slice(hd * MLA_V_DIM, (hd + 1) * MLA_V_DIM)

        def acc_rows(hd):
            return slice(hd * ACC_ROWS, (hd + 1) * ACC_ROWS)

        def first_task(hd):
            def score():
                km = kb0_ref[0, front:BLK, hd * LANES:(hd + 1) * LANES]
                return _nt(kblk(j, hd), qh(hd)), _nt(km, qh(hd))

            def finish(ss):
                sd = jnp.where(diag_mask, ss[0], NEG)
                sm = jnp.where(j >= 1, ss[1], NEG)
                m = jnp.maximum(jnp.max(sd, axis=0, keepdims=True), jnp.max(sm, axis=0, keepdims=True))
                pd = jnp.exp2(sd - m).astype(BF16)
                pm = jnp.exp2(sm - m).astype(BF16)
                m_s[hd:hd + 1, :] = m
                vd = with_ones(vbt_ref[0, j, v_rows(hd), :])
                vm = with_ones(vbt0_ref[0, 0, v_rows(hd), front:BLK])
                acc_s[acc_rows(hd), :] = (jnp.dot(vd, pd, preferred_element_type=F32)
                                          + jnp.dot(vm, pm, preferred_element_type=F32))

            return score, finish

        def block_score(i, hd, slot):
            s_buf[slot] = _nt(kblk(i, hd), qh(hd))

        def block_finish(i, hd, slot):
            s = s_buf[slot]
            m_old = m_s[hd:hd + 1, :]
            m_new = jnp.maximum(m_old, jnp.max(s, axis=0, keepdims=True))
            alpha = jnp.exp2(m_old - m_new)
            p = jnp.exp2(s - m_new).astype(BF16)
            m_s[hd:hd + 1, :] = m_new
            vi = with_ones(vbt_ref[0, i, v_rows(hd), :])
            acc_s[acc_rows(hd), :] = acc_s[acc_rows(hd), :] * alpha + jnp.dot(
                vi, p, preferred_element_type=F32)

        def lookahead_task(hd):
            return (lambda: _nt(kb_ref[0, BLK:2 * BLK, hd * LANES:(hd + 1) * LANES], qh(hd)),
                    lambda s: s_buf.__setitem__(hd, s))

        tasks = [lookahead_task(hd) for hd in range(PIPE_DEPTH)]
        tasks += [swa_task(pair, hq, g) for pair in ((0, 1), (2, 3)) for hq in range(2)
                  for g in range(SWA_KV_HEADS)]
        tasks += [first_task(hd) for hd in range(MLA_HEADS)]
        _run_pipelined(tasks, PIPE_DEPTH)

        def run_blocks(first_blk, nb):
            for t in range(nb * MLA_HEADS):
                la = t + PIPE_DEPTH
                block_score(first_blk + la // MLA_HEADS, la % MLA_HEADS, la % S_SLOTS)
                block_finish(first_blk + t // MLA_HEADS, t % MLA_HEADS, t % S_SLOTS)

        n_blocks = jnp.maximum(j - 1, 0)
        n_pairs = lax.shift_right_logical(n_blocks, 1)

        def pair_body(k, carry):
            run_blocks(1 + 2 * k, 2)
            return carry

        def single_body(k, carry):
            run_blocks(1 + 2 * n_pairs, 1)
            return carry

        lax.fori_loop(0, n_pairs, pair_body, 0)
        lax.fori_loop(0, n_blocks & 1, single_body, 0)

        for hd in range(MLA_HEADS):
            acc = acc_s[acc_rows(hd), :]
            o = acc[:MLA_V_DIM] * (1.0 / acc[MLA_V_DIM:MLA_V_DIM + 1])
            o_ref[0, 0, SWA_Q_W + hd * MLA_V_DIM:SWA_Q_W + (hd + 1) * MLA_V_DIM, :] = o.astype(o_ref.dtype)

    return attn_kernel


def _post_body(r, h, o_ref, gmix_ref, wo_ref, gffn_ref, wg_ref, wu_ref, wd_ref):
    a = o_ref[r, 0].astype(F32)
    aa, bb = a[:SWA_Q_W], a[SWA_Q_W:]
    ia = lax.rsqrt(jnp.mean(aa * aa, axis=0, keepdims=True) + EPS)
    ib = lax.rsqrt(jnp.mean(bb * bb, axis=0, keepdims=True) + EPS)
    y = jnp.concatenate([aa * ia, bb * ib], axis=0)
    mix = (y.T * gmix_ref[...]).astype(BF16)
    h1 = h + jnp.dot(mix, wo_ref[...], preferred_element_type=F32)
    u = (_rms(h1) * gffn_ref[...]).astype(BF16)
    g = jnp.dot(u, wg_ref[...], preferred_element_type=F32)
    up = jnp.dot(u, wu_ref[...], preferred_element_type=F32)
    act = (g * jax.nn.sigmoid(g) * up).astype(BF16)
    return h1 + jnp.dot(act, wd_ref[...], preferred_element_type=F32)


N_POST_W = 6
N_PRE_W = 8


def _block_is_needed():
    return (pl.program_id(0) > 0) | (pl.program_id(1) == 0)


def _zero_skipped(out_refs):
    @pl.when(jnp.logical_not(_block_is_needed()))
    def _():
        for ref in out_refs:
            ref[...] = jnp.zeros(ref.shape, ref.dtype)


def _first_kernel(front, x_ref, meta_ref, *refs):
    pre_w, h_out, pre_out = refs[:N_PRE_W], refs[N_PRE_W], refs[N_PRE_W + 1:]

    @pl.when(_block_is_needed())
    def _():
        meta_blk = jnp.concatenate([jnp.zeros((front, meta_ref.shape[1]), F32), meta_ref[...]],
                                   axis=0)
        for r in range(ROWS):
            h = jnp.where(pl.program_id(0) == 0, meta_blk, x_ref[r])
            h_out[r] = h
            _pre_body(r, h, *pre_w, *pre_out)

    _zero_skipped((h_out,) + tuple(pre_out))


def _mid_kernel(h_ref, o_ref, *refs):
    post_w, pre_w = refs[:N_POST_W], refs[N_POST_W:N_POST_W + N_PRE_W]
    h_out, pre_out = refs[N_POST_W + N_PRE_W], refs[N_POST_W + N_PRE_W + 1:]

    @pl.when(_block_is_needed())
    def _():
        hs = [_post_body(r, h_ref[r], o_ref, *post_w) for r in range(ROWS)]
        for r in range(ROWS):
            h_out[r] = hs[r]
            _pre_body(r, hs[r], *pre_w, *pre_out)

    _zero_skipped((h_out,) + tuple(pre_out))


def _last_kernel(h_ref, o_ref, *refs):
    post_w, gfinal_ref, out_ref = refs[:N_POST_W], refs[N_POST_W], refs[N_POST_W + 1]
    for r in range(ROWS):
        h = _post_body(r, h_ref[r], o_ref, *post_w)
        out_ref[r] = _rms(h) * gfinal_ref[...]


def _gather_cols(w, idx):
    idx = list(idx)
    parts, i = [], 0
    while i < len(idx):
        k = i + 1
        if idx[i] < 0:
            while k < len(idx) and idx[k] < 0:
                k += 1
            parts.append(jnp.zeros(w.shape[:-1] + (k - i,), w.dtype))
        else:
            while k < len(idx) and idx[k] == idx[k - 1] + 1:
                k += 1
            parts.append(w[..., idx[i]:idx[i] + (k - i)])
        i = k
    return jnp.concatenate(parts, axis=-1)


def _w1_index(o1, o2, o3, o4, o5):
    half = SWA_HEAD_DIM // 2
    idx = []
    for i in range(SWA_GROUP):
        pair = (i, i + SWA_GROUP)
        for part in range(2):
            for hd in pair:
                idx += [hd * SWA_HEAD_DIM + part * half + t for t in range(half)]
    for part in range(2):
        for g in range(SWA_KV_HEADS):
            idx += [o1 + g * SWA_HEAD_DIM + part * half + t for t in range(half)]
    idx += list(range(o2, o5))
    idx += _mla_group_index(o5, None)
    return idx


def _mla_group_index(rope_base, nope_base):
    hn, hr = MLA_NOPE_DIM // 2, MLA_ROPE_DIM // 2
    pad = LANES // 2 - hn - hr
    idx = []
    for part in range(2):
        idx += [-1] * hn if nope_base is None else [nope_base + part * hn + t for t in range(hn)]
        idx += [-1] * hr if rope_base is None else [rope_base + part * hr + t for t in range(hr)]
        idx += [-1] * pad
    return idx


def _rope_tables(tp, front):
    pos = (jnp.arange(tp) - front).astype(F32)

    def cs(d):
        inv = ROPE_THETA ** (-jnp.arange(0, d, 2, dtype=F32) / d)
        ang = pos[:, None] * inv[None, :]
        return jnp.cos(ang), jnp.sin(ang)

    ca, sa = cs(SWA_HEAD_DIM)
    CA = jnp.concatenate([ca] * 4, axis=1)
    SA = jnp.concatenate([-sa, -sa, sa, sa], axis=1)
    cb, sb = cs(MLA_ROPE_DIM)
    hn, hr = MLA_NOPE_DIM // 2, MLA_ROPE_DIM // 2
    pad = LANES // 2 - hn - hr
    one_n, one_p = jnp.ones((tp, hn), F32), jnp.ones((tp, pad), F32)
    z_n, z_p = jnp.zeros((tp, hn), F32), jnp.zeros((tp, pad), F32)
    CB = jnp.concatenate([one_n, cb, one_p, one_n, cb, one_p], axis=1)
    SB = jnp.concatenate([z_n, -sb, z_p, z_n, sb, z_p], axis=1)
    return jnp.concatenate([CA, SA, CB, SB], axis=1)


def _const_spec(shape):
    return pl.BlockSpec(shape, lambda *_: (0,) * len(shape), pipeline_mode=pl.Buffered(1))


def kernel(x, meta_tokens, attn_norm, w_in, q_norm, w_q_up, kv_norm, w_kv_up, sinks,
           out_norm_swa, out_norm_mla, w_o, ffn_norm, w_gate, w_up, w_down, final_norm):
    B, S, D = x.shape
    depth = w_in.shape[0]
    d_ff = w_gate.shape[-1]
    assert S % BLK == 0 and N_META <= BLK and WINDOW <= HALF and MLA_HEADS % S_SLOTS == 0 and B % ROWS == 0
    front = BLK - N_META
    tp = BLK + S
    nblk = tp // BLK

    o1 = SWA_Q_W
    o2 = o1 + SWA_KV_W
    o3 = o2 + SWA_KV_W
    o4 = o3 + MLA_Q_RANK
    o5 = o4 + MLA_KV_RANK

    w1 = _gather_cols(w_in, _w1_index(o1, o2, o3, o4, o5)).astype(BF16)
    wq_idx, wkn_idx, wv_idx = [], [], []
    for hd in range(MLA_HEADS):
        wq_idx += _mla_group_index(hd * MLA_QK_DIM + MLA_NOPE_DIM, hd * MLA_QK_DIM)
        wkn_idx += _mla_group_index(None, hd * (MLA_NOPE_DIM + MLA_V_DIM))
        wv_idx += [hd * (MLA_NOPE_DIM + MLA_V_DIM) + MLA_NOPE_DIM + t for t in range(MLA_V_DIM)]
    wq = _gather_cols(w_q_up, wq_idx).astype(BF16)
    wkn = _gather_cols(w_kv_up, wkn_idx).astype(BF16)
    wvt = jnp.swapaxes(_gather_cols(w_kv_up, wv_idx), 1, 2).astype(BF16)
    wo = w_o.astype(BF16)
    wg = w_gate.astype(BF16)
    wu = w_up.astype(BF16)
    wd = w_down.astype(BF16)
    gmix = jnp.concatenate([out_norm_swa, out_norm_mla], axis=-1)
    tabs = _rope_tables(tp, front)

    hw = MLA_HEADS * LANES
    h_spec = pl.BlockSpec((ROWS, BLK, D), lambda j, b: (b, j, 0))
    o_spec = pl.BlockSpec((ROWS, 1, MIX_W, BLK), lambda j, b: (b, j, 0, 0))
    pre_w_specs = [
        pl.BlockSpec((BLK, N_TABS * LANES), lambda j, b: (j, 0)),
        _const_spec((1, D)),
        _const_spec((D, W1_COLS)),
        _const_spec((1, MLA_Q_RANK)),
        _const_spec((MLA_Q_RANK, hw)),
        _const_spec((1, MLA_KV_RANK)),
        _const_spec((MLA_KV_RANK, hw)),
        _const_spec((MLA_OUT_W, MLA_KV_RANK)),
    ]
    post_w_specs = [
        _const_spec((1, MIX_W)),
        _const_spec((MIX_W, D)),
        _const_spec((1, D)),
        _const_spec((D, d_ff)),
        _const_spec((D, d_ff)),
        _const_spec((d_ff, D)),
    ]
    pre_out_specs = [
        pl.BlockSpec((ROWS, BLK, SWA_Q_W + hw), lambda j, b: (b, j, 0)),
        pl.BlockSpec((ROWS, BLK, LANES), lambda j, b: (b, j, 0)),
        pl.BlockSpec((ROWS, 2, LANES, HALF), lambda j, b: (b, j, 0, 0)),
        pl.BlockSpec((ROWS, BLK, hw), lambda j, b: (b, j, 0)),
        pl.BlockSpec((ROWS, 1, MLA_OUT_W, BLK), lambda j, b: (b, j, 0, 0)),
    ]
    pre_out_shapes = [
        jax.ShapeDtypeStruct((B, tp, SWA_Q_W + hw), BF16),
        jax.ShapeDtypeStruct((B, tp, LANES), BF16),
        jax.ShapeDtypeStruct((B, 2 * nblk, LANES, HALF), BF16),
        jax.ShapeDtypeStruct((B, tp, hw), BF16),
        jax.ShapeDtypeStruct((B, nblk, MLA_OUT_W, BLK), BF16),
    ]
    h_shape = jax.ShapeDtypeStruct((B, tp, D), F32)
    seq_params = dict(dimension_semantics=("arbitrary", "arbitrary"))

    def pre_w(l):
        return (tabs, attn_norm[l][None], w1[l], q_norm[l][None], wq[l], kv_norm[l][None],
                wkn[l], wvt[l])

    def post_w(l):
        return (gmix[l][None], wo[l], ffn_norm[l][None], wg[l], wu[l], wd[l])

    first_call = pl.pallas_call(
        functools.partial(_first_kernel, front),
        grid=(nblk, B // ROWS),
        in_specs=[pl.BlockSpec((ROWS, BLK, D), lambda j, b: (b, jnp.maximum(j - 1, 0), 0)),
                  _const_spec((N_META, D))] + pre_w_specs,
        out_specs=[h_spec] + pre_out_specs,
        out_shape=[h_shape] + pre_out_shapes,
        compiler_params=pltpu.CompilerParams(vmem_limit_bytes=40 * MIB, **seq_params),
        name="first",
    )

    attn_call = pl.pallas_call(
        _make_attn_kernel(front),
        grid=(B, nblk),
        in_specs=[
            pl.BlockSpec(memory_space=pltpu.SMEM),
            pl.BlockSpec((1, BLK, SWA_Q_W + hw), lambda b, j: (b, j, 0)),
            pl.BlockSpec((1, tp, LANES), lambda b, j: (b, 0, 0)),
            pl.BlockSpec((1, 2 * nblk, LANES, HALF), lambda b, j: (b, 0, 0, 0)),
            pl.BlockSpec((1, tp, hw), lambda b, j: (b, 0, 0)),
            pl.BlockSpec((1, nblk, MLA_OUT_W, BLK), lambda b, j: (b, 0, 0, 0)),
            _const_spec((1, BLK, LANES)),
            _const_spec((1, 2, LANES, HALF)),
            _const_spec((1, BLK, hw)),
            _const_spec((1, 1, MLA_OUT_W, BLK)),
        ],
        out_specs=pl.BlockSpec((1, 1, MIX_W, BLK), lambda b, j: (b, j, 0, 0)),
        out_shape=jax.ShapeDtypeStruct((B, nblk, MIX_W, BLK), BF16),
        scratch_shapes=[
            pltpu.VMEM((MLA_HEADS, BLK), F32),
            pltpu.VMEM((MLA_HEADS * ACC_ROWS, BLK), F32),
            pltpu.VMEM((S_SLOTS, BLK, BLK), F32),
        ],
        compiler_params=pltpu.CompilerParams(vmem_limit_bytes=48 * MIB, **seq_params),
        name="attn",
    )

    mid_call = pl.pallas_call(
        _mid_kernel,
        grid=(nblk, B // ROWS),
        in_specs=[h_spec, o_spec] + post_w_specs + pre_w_specs,
        out_specs=[h_spec] + pre_out_specs,
        out_shape=[h_shape] + pre_out_shapes,
        compiler_params=pltpu.CompilerParams(vmem_limit_bytes=60 * MIB, **seq_params),
        name="mid",
    )

    last_call = pl.pallas_call(
        _last_kernel,
        grid=(B // ROWS, S // BLK),
        in_specs=[pl.BlockSpec((ROWS, BLK, D), lambda b, j: (b, j + 1, 0)),
                  pl.BlockSpec((ROWS, 1, MIX_W, BLK), lambda b, j: (b, j + 1, 0, 0))]
                 + post_w_specs + [_const_spec((1, D))],
        out_specs=pl.BlockSpec((ROWS, BLK, D), lambda b, j: (b, j, 0)),
        out_shape=jax.ShapeDtypeStruct((B, S, D), x.dtype),
        compiler_params=pltpu.CompilerParams(vmem_limit_bytes=56 * MIB, **seq_params),
        name="last",
    )

    h, *pre_out = first_call(x, meta_tokens.astype(F32), *pre_w(0))
    for l in range(depth):
        q, ka, vat, kb, vbt = pre_out
        o_t = attn_call(sinks[l], q, ka, vat, kb, vbt, ka, vat, kb, vbt)
        if l + 1 < depth:
            h, *pre_out = mid_call(h, o_t, *post_w(l), *pre_w(l + 1))
    return last_call(h, o_t, *post_w(depth - 1), final_norm[None])
```

```python
import functools

import jax
import jax.numpy as jnp
from jax import lax
from jax.experimental import pallas as pl
from jax.experimental.pallas import tpu as pltpu

N_META = 16
WINDOW = 128
ROPE_THETA = 10000.0
EPS = 1e-6
NEG = -1e30
SWA_HEADS = 8
SWA_KV_HEADS = 2
SWA_HEAD_DIM = 64
SWA_GROUP = SWA_HEADS // SWA_KV_HEADS
MLA_HEADS = 8
MLA_Q_RANK = 256
MLA_KV_RANK = 128
MLA_NOPE_DIM = 64
MLA_ROPE_DIM = 32
MLA_V_DIM = 64
MLA_QK_DIM = MLA_NOPE_DIM + MLA_ROPE_DIM
SWA_Q_W = SWA_HEADS * SWA_HEAD_DIM
SWA_KV_W = SWA_KV_HEADS * SWA_HEAD_DIM
MLA_OUT_W = MLA_HEADS * MLA_V_DIM
MIX_W = SWA_Q_W + MLA_OUT_W

LANES = 128
BLK = 256
HALF = BLK // 2
MIB = 1024 * 1024
PIPE_DEPTH = 7
S_SLOTS = PIPE_DEPTH + 1
ONES_ROWS = 16
ACC_ROWS = MLA_V_DIM + ONES_ROWS
LOG2E = 1.4426950408889634
ROWS = 2

C_QA = 0
C_KA = C_QA + SWA_Q_W
C_VA = C_KA + SWA_KV_W
C_QL = C_VA + SWA_KV_W
C_KVL = C_QL + MLA_Q_RANK
C_KR = C_KVL + MLA_KV_RANK
W1_COLS = C_KR + LANES

T_CA, T_SA, T_CB, T_SB = range(4)
N_TABS = 4
QA_SCALE = SWA_HEAD_DIM ** -0.5 * LOG2E
QB_SCALE = MLA_QK_DIM ** -0.5 * LOG2E

F32 = jnp.float32
BF16 = jnp.bfloat16


def _rms(x):
    return x * lax.rsqrt(jnp.mean(x * x, axis=-1, keepdims=True) + EPS)


def _nt(a, b):
    return lax.dot_general(a, b, (((1,), (1,)), ((), ())), preferred_element_type=F32)


def _rope(x, c, s):
    return x * c + pltpu.roll(x, LANES // 2, 1) * s


def _pre_body(r, h, tab_ref, gattn_ref, w1_ref, gq_ref, wq_ref, gkv_ref, wkn_ref, wvt_ref,
              q_ref, ka_ref, vat_ref, kb_ref, vbt_ref):
    def tab(i):
        return tab_ref[:, i * LANES:(i + 1) * LANES]

    u = (_rms(h) * gattn_ref[...]).astype(BF16)
    proj = jnp.dot(u, w1_ref[...], preferred_element_type=F32)

    caq, saq = tab(T_CA) * QA_SCALE, tab(T_SA) * QA_SCALE
    for i in range(SWA_Q_W // LANES):
        sl = slice(C_QA + i * LANES, C_QA + (i + 1) * LANES)
        q_ref[r, :, i * LANES:(i + 1) * LANES] = _rope(proj[:, sl], caq, saq).astype(BF16)
    ka_ref[r] = _rope(proj[:, C_KA:C_KA + LANES], tab(T_CA), tab(T_SA)).astype(BF16)
    vat = proj[:, C_VA:C_VA + LANES].T
    vat_ref[r, 0] = vat[:, :HALF].astype(BF16)
    vat_ref[r, 1] = vat[:, HALF:].astype(BF16)

    qln = (_rms(proj[:, C_QL:C_QL + MLA_Q_RANK]) * gq_ref[...]).astype(BF16)
    qb = jnp.dot(qln, wq_ref[...], preferred_element_type=F32)
    cbq, sbq = tab(T_CB) * QB_SCALE, tab(T_SB) * QB_SCALE
    for hd in range(MLA_HEADS):
        sl = slice(hd * LANES, (hd + 1) * LANES)
        q_ref[r, :, SWA_Q_W + hd * LANES:SWA_Q_W + (hd + 1) * LANES] = _rope(
            qb[:, sl], cbq, sbq).astype(BF16)

    kvn = _rms(proj[:, C_KVL:C_KVL + MLA_KV_RANK]) * gkv_ref[...]
    kn = jnp.dot(kvn.astype(BF16), wkn_ref[...], preferred_element_type=F32)
    kr = _rope(proj[:, C_KR:C_KR + LANES], tab(T_CB), tab(T_SB))
    for hd in range(MLA_HEADS):
        sl = slice(hd * LANES, (hd + 1) * LANES)
        kb_ref[r, :, sl] = (kn[:, sl] + kr).astype(BF16)
    kvn_t = kvn.T.astype(BF16)
    vbt_ref[r, 0] = jnp.dot(wvt_ref[...], kvn_t, preferred_element_type=F32).astype(BF16)


def _run_pipelined(tasks, depth):
    pending = []
    for score_fn, finish_fn in tasks:
        pending.append((finish_fn, score_fn()))
        if len(pending) > depth:
            fin, s = pending.pop(0)
            fin(s)
    for fin, s in pending:
        fin(s)


def _make_attn_kernel(front):
    def attn_kernel(sinks_ref, q_ref, ka_ref, vat_ref, kb_ref, vbt_ref, ka0_ref, vat0_ref, kb0_ref,
                    vbt0_ref, o_ref, m_s, acc_s, s_buf):
        needed = (pl.program_id(1) > 0) | (pl.program_id(0) < ROWS)
        pl.when(needed)(
            functools.partial(attn_body, sinks_ref, q_ref, ka_ref, vat_ref, kb_ref, vbt_ref, ka0_ref,
                              vat0_ref, kb0_ref, vbt0_ref, o_ref, m_s, acc_s, s_buf))

        @pl.when(jnp.logical_not(needed))
        def _():
            o_ref[...] = jnp.zeros(o_ref.shape, o_ref.dtype)

    def attn_body(sinks_ref, q_ref, ka_ref, vat_ref, kb_ref, vbt_ref, ka0_ref, vat0_ref, kb0_ref,
                  vbt0_ref, o_ref, m_s, acc_s, s_buf):
        j = pl.program_id(1)

        def with_ones(v):
            return jnp.concatenate([v, jnp.ones((ONES_ROWS, v.shape[1]), v.dtype)], axis=0)

        prev_half = jnp.maximum(2 * j - 1, 0)
        k_prev = jnp.where(j == 1, ka0_ref[0, HALF:, :],
                           ka_ref[0, pl.ds(pl.multiple_of(prev_half * HALF, HALF), HALF), :])
        v_prev = jnp.where(j == 1, vat0_ref[0, 1], vat_ref[0, prev_half])
        kwin = jnp.concatenate(
            [k_prev, ka_ref[0, pl.ds(pl.multiple_of(j * BLK, BLK), BLK), :]], axis=0)
        lane = lax.broadcasted_iota(jnp.int32, (1, LANES), 1)
        first_kv = (lane % (LANES // 2)) < (SWA_HEAD_DIM // 2)
        zero = jnp.zeros_like(kwin)
        kg = (jnp.where(first_kv, kwin, zero), jnp.where(first_kv, zero, kwin))
        vwin = jnp.concatenate([v_prev, vat_ref[0, 2 * j], vat_ref[0, 2 * j + 1]], axis=1)
        vg_aug = [with_ones(vwin[g * SWA_HEAD_DIM:(g + 1) * SWA_HEAD_DIM, :])
                  for g in range(SWA_KV_HEADS)]
        rr = lax.broadcasted_iota(jnp.int32, (BLK, BLK), 0)
        cc = lax.broadcasted_iota(jnp.int32, (BLK, BLK), 1)
        d = (cc % HALF) + HALF - rr
        band = (d >= 0) & (d < WINDOW)
        swa_valid = [band & (rr + (j * BLK - HALF + hq * HALF) >= front) for hq in range(2)]
        first_head_cols = lax.broadcasted_iota(jnp.int32, (1, BLK), 1) < HALF

        def swa_task(pair, hq, g):
            hd0, hd1 = pair[0] + SWA_GROUP * g, pair[1] + SWA_GROUP * g
            keys = slice(hq * HALF, hq * HALF + BLK)
            cols = slice(hq * HALF, (hq + 1) * HALF)

            def score():
                w = jnp.concatenate([q_ref[0, cols, i * LANES:(i + 1) * LANES] for i in pair],
                                    axis=0)
                return _nt(kg[g][keys], w)

            def finish(s):
                s = jnp.where(swa_valid[hq], s, NEG)
                sink = jnp.where(first_head_cols, sinks_ref[hd0], sinks_ref[hd1]) * LOG2E
                m = jnp.maximum(jnp.max(s, axis=0, keepdims=True), sink)
                p = jnp.exp2(s - m).astype(BF16)
                o = jnp.dot(vg_aug[g][:, keys], p, preferred_element_type=F32)
                l = o[SWA_HEAD_DIM:SWA_HEAD_DIM + 1] + jnp.exp2(sink - m)
                on = (o[:SWA_HEAD_DIM] * (1.0 / l)).astype(o_ref.dtype)
                o_ref[0, 0, hd0 * SWA_HEAD_DIM:(hd0 + 1) * SWA_HEAD_DIM, cols] = on[:, :HALF]
                o_ref[0, 0, hd1 * SWA_HEAD_DIM:(hd1 + 1) * SWA_HEAD_DIM, cols] = on[:, HALF:]

            return score, finish

        def qh(hd):
            return q_ref[0, :, SWA_Q_W + hd * LANES:SWA_Q_W + (hd + 1) * LANES]

        def kblk(i, hd):
            return kb_ref[0, pl.ds(pl.multiple_of(i * BLK, BLK), BLK), hd * LANES:(hd + 1) * LANES]

        diag_mask = (rr <= cc) & (rr + j * BLK >= front)

        def v_rows(hd):
            return slice(hd * MLA_V_DIM, (hd + 1) * MLA_V_DIM)

        def acc_rows(hd):
            return slice(hd * ACC_ROWS, (hd + 1) * ACC_ROWS)

        def first_task(hd):
            def score():
                km = kb0_ref[0, front:BLK, hd * LANES:(hd + 1) * LANES]
                return _nt(kblk(j, hd), qh(hd)), _nt(km, qh(hd))

            def finish(ss):
                sd = jnp.where(diag_mask, ss[0], NEG)
                sm = jnp.where(j >= 1, ss[1], NEG)
                m = jnp.maximum(jnp.max(sd, axis=0, keepdims=True), jnp.max(sm, axis=0, keepdims=True))
                pd = jnp.exp2(sd - m).astype(BF16)
                pm = jnp.exp2(sm - m).astype(BF16)
                m_s[hd:hd + 1, :] = m
                vd = with_ones(vbt_ref[0, j, v_rows(hd), :])
                vm = with_ones(vbt0_ref[0, 0, v_rows(hd), front:BLK])
                acc_s[acc_rows(hd), :] = (jnp.dot(vd, pd, preferred_element_type=F32)
                                          + jnp.dot(vm, pm, preferred_element_type=F32))

            return score, finish

        def block_score(i, hd, slot):
            s_buf[slot] = _nt(kblk(i, hd), qh(hd))

        def block_finish(i, hd, slot):
            s = s_buf[slot]
            m_old = m_s[hd:hd + 1, :]
            m_new = jnp.maximum(m_old, jnp.max(s, axis=0, keepdims=True))
            alpha = jnp.exp2(m_old - m_new)
            p = jnp.exp2(s - m_new).astype(BF16)
            m_s[hd:hd + 1, :] = m_new
            vi = with_ones(vbt_ref[0, i, v_rows(hd), :])
            acc_s[acc_rows(hd), :] = acc_s[acc_rows(hd), :] * alpha + jnp.dot(
                vi, p, preferred_element_type=F32)

        def lookahead_task(hd):
            return (lambda: _nt(kb_ref[0, BLK:2 * BLK, hd * LANES:(hd + 1) * LANES], qh(hd)),
                    lambda s: s_buf.__setitem__(hd, s))

        tasks = [lookahead_task(hd) for hd in range(PIPE_DEPTH)]
        tasks += [swa_task(pair, hq, g) for pair in ((0, 1), (2, 3)) for hq in range(2)
                  for g in range(SWA_KV_HEADS)]
        tasks += [first_task(hd) for hd in range(MLA_HEADS)]
        _run_pipelined(tasks, PIPE_DEPTH)

        def run_blocks(first_blk, nb):
            for t in range(nb * MLA_HEADS):
                la = t + PIPE_DEPTH
                block_score(first_blk + la // MLA_HEADS, la % MLA_HEADS, la % S_SLOTS)
                block_finish(first_blk + t // MLA_HEADS, t % MLA_HEADS, t % S_SLOTS)

        n_blocks = jnp.maximum(j - 1, 0)
        n_pairs = lax.shift_right_logical(n_blocks, 1)

        def pair_body(k, carry):
            run_blocks(1 + 2 * k, 2)
            return carry

        def single_body(k, carry):
            run_blocks(1 + 2 * n_pairs, 1)
            return carry

        lax.fori_loop(0, n_pairs, pair_body, 0)
        lax.fori_loop(0, n_blocks & 1, single_body, 0)

        for hd in range(MLA_HEADS):
            acc = acc_s[acc_rows(hd), :]
            o = acc[:MLA_V_DIM] * (1.0 / acc[MLA_V_DIM:MLA_V_DIM + 1])
            o_ref[0, 0, SWA_Q_W + hd * MLA_V_DIM:SWA_Q_W + (hd + 1) * MLA_V_DIM, :] = o.astype(o_ref.dtype)

    return attn_kernel


def _post_body(r, h, o_ref, gmix_ref, wo_ref, gffn_ref, wg_ref, wu_ref, wd_ref):
    a = o_ref[r, 0].astype(F32)
    aa, bb = a[:SWA_Q_W], a[SWA_Q_W:]
    ia = lax.rsqrt(jnp.mean(aa * aa, axis=0, keepdims=True) + EPS)
    ib = lax.rsqrt(jnp.mean(bb * bb, axis=0, keepdims=True) + EPS)
    y = jnp.concatenate([aa * ia, bb * ib], axis=0)
    mix = (y.T * gmix_ref[...]).astype(BF16)
    h1 = h + jnp.dot(mix, wo_ref[...], preferred_element_type=F32)
    u = (_rms(h1) * gffn_ref[...]).astype(BF16)
    g = jnp.dot(u, wg_ref[...], preferred_element_type=F32)
    up = jnp.dot(u, wu_ref[...], preferred_element_type=F32)
    act = (g * jax.nn.sigmoid(g) * up).astype(BF16)
    return h1 + jnp.dot(act, wd_ref[...], preferred_element_type=F32)


N_POST_W = 6
N_PRE_W = 8


def _block_is_needed():
    return (pl.program_id(0) > 0) | (pl.program_id(1) == 0)


def _zero_skipped(out_refs):
    @pl.when(jnp.logical_not(_block_is_needed()))
    def _():
        for ref in out_refs:
            ref[...] = jnp.zeros(ref.shape, ref.dtype)


def _first_kernel(front, x_ref, meta_ref, *refs):
    pre_w, h_out, pre_out = refs[:N_PRE_W], refs[N_PRE_W], refs[N_PRE_W + 1:]

    @pl.when(_block_is_needed())
    def _():
        meta_blk = jnp.concatenate([jnp.zeros((front, meta_ref.shape[1]), F32), meta_ref[...]],
                                   axis=0)
        for r in range(ROWS):
            h = jnp.where(pl.program_id(0) == 0, meta_blk, x_ref[r])
            h_out[r] = h
            _pre_body(r, h, *pre_w, *pre_out)

    _zero_skipped((h_out,) + tuple(pre_out))


def _mid_kernel(h_ref, o_ref, *refs):
    post_w, pre_w = refs[:N_POST_W], refs[N_POST_W:N_POST_W + N_PRE_W]
    h_out, pre_out = refs[N_POST_W + N_PRE_W], refs[N_POST_W + N_PRE_W + 1:]

    @pl.when(_block_is_needed())
    def _():
        hs = [_post_body(r, h_ref[r], o_ref, *post_w) for r in range(ROWS)]
        for r in range(ROWS):
            h_out[r] = hs[r]
            _pre_body(r, hs[r], *pre_w, *pre_out)

    _zero_skipped((h_out,) + tuple(pre_out))


def _last_kernel(h_ref, o_ref, *refs):
    post_w, gfinal_ref, out_ref = refs[:N_POST_W], refs[N_POST_W], refs[N_POST_W + 1]
    for r in range(ROWS):
        h = _post_body(r, h_ref[r], o_ref, *post_w)
        out_ref[r] = _rms(h) * gfinal_ref[...]


def _gather_cols(w, idx):
    idx = list(idx)
    parts, i = [], 0
    while i < len(idx):
        k = i + 1
        if idx[i] < 0:
            while k < len(idx) and idx[k] < 0:
                k += 1
            parts.append(jnp.zeros(w.shape[:-1] + (k - i,), w.dtype))
        else:
            while k < len(idx) and idx[k] == idx[k - 1] + 1:
                k += 1
            parts.append(w[..., idx[i]:idx[i] + (k - i)])
        i = k
    return jnp.concatenate(parts, axis=-1)


def _w1_index(o1, o2, o3, o4, o5):
    half = SWA_HEAD_DIM // 2
    idx = []
    for i in range(SWA_GROUP):
        pair = (i, i + SWA_GROUP)
        for part in range(2):
            for hd in pair:
                idx += [hd * SWA_HEAD_DIM + part * half + t for t in range(half)]
    for part in range(2):
        for g in range(SWA_KV_HEADS):
            idx += [o1 + g * SWA_HEAD_DIM + part * half + t for t in range(half)]
    idx += list(range(o2, o5))
    idx += _mla_group_index(o5, None)
    return idx


def _mla_group_index(rope_base, nope_base):
    hn, hr = MLA_NOPE_DIM // 2, MLA_ROPE_DIM // 2
    pad = LANES // 2 - hn - hr
    idx = []
    for part in range(2):
        idx += [-1] * hn if nope_base is None else [nope_base + part * hn + t for t in range(hn)]
        idx += [-1] * hr if rope_base is None else [rope_base + part * hr + t for t in range(hr)]
        idx += [-1] * pad
    return idx


def _rope_tables(tp, front):
    pos = (jnp.arange(tp) - front).astype(F32)

    def cs(d):
        inv = ROPE_THETA ** (-jnp.arange(0, d, 2, dtype=F32) / d)
        ang = pos[:, None] * inv[None, :]
        return jnp.cos(ang), jnp.sin(ang)

    ca, sa = cs(SWA_HEAD_DIM)
    CA = jnp.concatenate([ca] * 4, axis=1)
    SA = jnp.concatenate([-sa, -sa, sa, sa], axis=1)
    cb, sb = cs(MLA_ROPE_DIM)
    hn, hr = MLA_NOPE_DIM // 2, MLA_ROPE_DIM // 2
    pad = LANES // 2 - hn - hr
    one_n, one_p = jnp.ones((tp, hn), F32), jnp.ones((tp, pad), F32)
    z_n, z_p = jnp.zeros((tp, hn), F32), jnp.zeros((tp, pad), F32)
    CB = jnp.concatenate([one_n, cb, one_p, one_n, cb, one_p], axis=1)
    SB = jnp.concatenate([z_n, -sb, z_p, z_n, sb, z_p], axis=1)
    return jnp.concatenate([CA, SA, CB, SB], axis=1)


def _const_spec(shape):
    return pl.BlockSpec(shape, lambda *_: (0,) * len(shape), pipeline_mode=pl.Buffered(1))


def _layer_spec(l, shape):
    return pl.BlockSpec((None,) + shape, lambda *_: (l,) + (0,) * len(shape),
                        pipeline_mode=pl.Buffered(1))


def kernel(x, meta_tokens, attn_norm, w_in, q_norm, w_q_up, kv_norm, w_kv_up, sinks,
           out_norm_swa, out_norm_mla, w_o, ffn_norm, w_gate, w_up, w_down, final_norm):
    B, S, D = x.shape
    depth = w_in.shape[0]
    d_ff = w_gate.shape[-1]
    assert S % BLK == 0 and N_META <= BLK and WINDOW <= HALF and MLA_HEADS % S_SLOTS == 0 and B % ROWS == 0
    front = BLK - N_META
    tp = BLK + S
    nblk = tp // BLK

    o1 = SWA_Q_W
    o2 = o1 + SWA_KV_W
    o3 = o2 + SWA_KV_W
    o4 = o3 + MLA_Q_RANK
    o5 = o4 + MLA_KV_RANK

    w1 = _gather_cols(w_in, _w1_index(o1, o2, o3, o4, o5)).astype(BF16)
    wq_idx, wkn_idx, wv_idx = [], [], []
    for hd in range(MLA_HEADS):
        wq_idx += _mla_group_index(hd * MLA_QK_DIM + MLA_NOPE_DIM, hd * MLA_QK_DIM)
        wkn_idx += _mla_group_index(None, hd * (MLA_NOPE_DIM + MLA_V_DIM))
        wv_idx += [hd * (MLA_NOPE_DIM + MLA_V_DIM) + MLA_NOPE_DIM + t for t in range(MLA_V_DIM)]
    wq = _gather_cols(w_q_up, wq_idx).astype(BF16)
    wkn = _gather_cols(w_kv_up, wkn_idx).astype(BF16)
    wvt = jnp.swapaxes(_gather_cols(w_kv_up, wv_idx), 1, 2).astype(BF16)
    wo = w_o.astype(BF16)
    wg = w_gate.astype(BF16)
    wu = w_up.astype(BF16)
    wd = w_down.astype(BF16)
    gmix = jnp.concatenate([out_norm_swa, out_norm_mla], axis=-1)
    tabs = _rope_tables(tp, front)

    hw = MLA_HEADS * LANES
    h_spec = pl.BlockSpec((ROWS, BLK, D), lambda j, b: (b, j, 0))
    o_spec = pl.BlockSpec((ROWS, 1, MIX_W, BLK), lambda j, b: (b, j, 0, 0))
    def pre_w_specs(l):
        return [
            pl.BlockSpec((BLK, N_TABS * LANES), lambda j, b: (j, 0)),
            _layer_spec(l, (1, D)),
            _layer_spec(l, (D, W1_COLS)),
            _layer_spec(l, (1, MLA_Q_RANK)),
            _layer_spec(l, (MLA_Q_RANK, hw)),
            _layer_spec(l, (1, MLA_KV_RANK)),
            _layer_spec(l, (MLA_KV_RANK, hw)),
            _layer_spec(l, (MLA_OUT_W, MLA_KV_RANK)),
        ]

    def post_w_specs(l):
        return [
            _layer_spec(l, (1, MIX_W)),
            _layer_spec(l, (MIX_W, D)),
            _layer_spec(l, (1, D)),
            _layer_spec(l, (D, d_ff)),
            _layer_spec(l, (D, d_ff)),
            _layer_spec(l, (d_ff, D)),
        ]
    pre_out_specs = [
        pl.BlockSpec((ROWS, BLK, SWA_Q_W + hw), lambda j, b: (b, j, 0)),
        pl.BlockSpec((ROWS, BLK, LANES), lambda j, b: (b, j, 0)),
        pl.BlockSpec((ROWS, 2, LANES, HALF), lambda j, b: (b, j, 0, 0)),
        pl.BlockSpec((ROWS, BLK, hw), lambda j, b: (b, j, 0)),
        pl.BlockSpec((ROWS, 1, MLA_OUT_W, BLK), lambda j, b: (b, j, 0, 0)),
    ]
    pre_out_shapes = [
        jax.ShapeDtypeStruct((B, tp, SWA_Q_W + hw), BF16),
        jax.ShapeDtypeStruct((B, tp, LANES), BF16),
        jax.ShapeDtypeStruct((B, 2 * nblk, LANES, HALF), BF16),
        jax.ShapeDtypeStruct((B, tp, hw), BF16),
        jax.ShapeDtypeStruct((B, nblk, MLA_OUT_W, BLK), BF16),
    ]
    h_shape = jax.ShapeDtypeStruct((B, tp, D), F32)
    seq_params = dict(dimension_semantics=("arbitrary", "arbitrary"))

    pre_w = (tabs, attn_norm[:, None], w1, q_norm[:, None], wq, kv_norm[:, None], wkn, wvt)
    post_w = (gmix[:, None], wo, ffn_norm[:, None], wg, wu, wd)

    first_call = pl.pallas_call(
        functools.partial(_first_kernel, front),
        grid=(nblk, B // ROWS),
        in_specs=[pl.BlockSpec((ROWS, BLK, D), lambda j, b: (b, jnp.maximum(j - 1, 0), 0)),
                  _const_spec((N_META, D))] + pre_w_specs(0),
        out_specs=[h_spec] + pre_out_specs,
        out_shape=[h_shape] + pre_out_shapes,
        compiler_params=pltpu.CompilerParams(vmem_limit_bytes=40 * MIB, **seq_params),
        name="first",
    )

    attn_call = pl.pallas_call(
        _make_attn_kernel(front),
        grid=(B, nblk),
        in_specs=[
            pl.BlockSpec(memory_space=pltpu.SMEM),
            pl.BlockSpec((1, BLK, SWA_Q_W + hw), lambda b, j: (b, j, 0)),
            pl.BlockSpec((1, tp, LANES), lambda b, j: (b, 0, 0)),
            pl.BlockSpec((1, 2 * nblk, LANES, HALF), lambda b, j: (b, 0, 0, 0)),
            pl.BlockSpec((1, tp, hw), lambda b, j: (b, 0, 0)),
            pl.BlockSpec((1, nblk, MLA_OUT_W, BLK), lambda b, j: (b, 0, 0, 0)),
            _const_spec((1, BLK, LANES)),
            _const_spec((1, 2, LANES, HALF)),
            _const_spec((1, BLK, hw)),
            _const_spec((1, 1, MLA_OUT_W, BLK)),
        ],
        out_specs=pl.BlockSpec((1, 1, MIX_W, BLK), lambda b, j: (b, j, 0, 0)),
        out_shape=jax.ShapeDtypeStruct((B, nblk, MIX_W, BLK), BF16),
        scratch_shapes=[
            pltpu.VMEM((MLA_HEADS, BLK), F32),
            pltpu.VMEM((MLA_HEADS * ACC_ROWS, BLK), F32),
            pltpu.VMEM((S_SLOTS, BLK, BLK), F32),
        ],
        compiler_params=pltpu.CompilerParams(vmem_limit_bytes=48 * MIB, **seq_params),
        name="attn",
    )

    def mid_call(l):
        return pl.pallas_call(
            _mid_kernel,
            grid=(nblk, B // ROWS),
            in_specs=[h_spec, o_spec] + post_w_specs(l) + pre_w_specs(l + 1),
            out_specs=[h_spec] + pre_out_specs,
            out_shape=[h_shape] + pre_out_shapes,
            compiler_params=pltpu.CompilerParams(vmem_limit_bytes=60 * MIB, **seq_params),
            name="mid",
        )

    last_call = pl.pallas_call(
        _last_kernel,
        grid=(B // ROWS, S // BLK),
        in_specs=[pl.BlockSpec((ROWS, BLK, D), lambda b, j: (b, j + 1, 0)),
                  pl.BlockSpec((ROWS, 1, MIX_W, BLK), lambda b, j: (b, j + 1, 0, 0))]
                 + post_w_specs(depth - 1) + [_const_spec((1, D))],
        out_specs=pl.BlockSpec((ROWS, BLK, D), lambda b, j: (b, j, 0)),
        out_shape=jax.ShapeDtypeStruct((B, S, D), x.dtype),
        compiler_params=pltpu.CompilerParams(vmem_limit_bytes=56 * MIB, **seq_params),
        name="last",
    )

    h, *pre_out = first_call(x, meta_tokens.astype(F32), *pre_w)
    for l in range(depth):
        q, ka, vat, kb, vbt = pre_out
        o_t = attn_call(sinks[l], q, ka, vat, kb, vbt, ka, vat, kb, vbt)
        if l + 1 < depth:
            h, *pre_out = mid_call(l)(h, o_t, *post_w, *pre_w)
    return last_call(h, o_t, *post_w, final_norm[None])
```

```python
import functools

import jax
import jax.numpy as jnp
from jax import lax
from jax.experimental import pallas as pl
from jax.experimental.pallas import tpu as pltpu

N_META = 16
WINDOW = 128
ROPE_THETA = 10000.0
EPS = 1e-6
NEG = -1e30
SWA_HEADS = 8
SWA_KV_HEADS = 2
SWA_HEAD_DIM = 64
SWA_GROUP = SWA_HEADS // SWA_KV_HEADS
MLA_HEADS = 8
MLA_Q_RANK = 256
MLA_KV_RANK = 128
MLA_NOPE_DIM = 64
MLA_ROPE_DIM = 32
MLA_V_DIM = 64
MLA_QK_DIM = MLA_NOPE_DIM + MLA_ROPE_DIM
SWA_Q_W = SWA_HEADS * SWA_HEAD_DIM
SWA_KV_W = SWA_KV_HEADS * SWA_HEAD_DIM
MLA_OUT_W = MLA_HEADS * MLA_V_DIM
MIX_W = SWA_Q_W + MLA_OUT_W

LANES = 128
BLK = 256
HALF = BLK // 2
MIB = 1024 * 1024
PIPE_DEPTH = 7
S_SLOTS = PIPE_DEPTH + 1
ONES_ROWS = 16
ACC_ROWS = MLA_V_DIM + ONES_ROWS
LOG2E = 1.4426950408889634
ROWS = 2

C_QA = 0
C_KA = C_QA + SWA_Q_W
C_VA = C_KA + SWA_KV_W
C_QL = C_VA + SWA_KV_W
C_KVL = C_QL + MLA_Q_RANK
C_KR = C_KVL + MLA_KV_RANK
W1_COLS = C_KR + LANES

T_CA, T_SA, T_CB, T_SB = range(4)
N_TABS = 4
QA_SCALE = SWA_HEAD_DIM ** -0.5 * LOG2E
QB_SCALE = MLA_QK_DIM ** -0.5 * LOG2E

F32 = jnp.float32
BF16 = jnp.bfloat16


def _rms(x):
    return x * lax.rsqrt(jnp.mean(x * x, axis=-1, keepdims=True) + EPS)


def _nt(a, b):
    return lax.dot_general(a, b, (((1,), (1,)), ((), ())), preferred_element_type=F32)


def _rope(x, c, s):
    return x * c + pltpu.roll(x, LANES // 2, 1) * s


def _pre_body(r, h, tab_ref, gattn_ref, w1_ref, gq_ref, wq_ref, gkv_ref, wkn_ref, wvt_ref,
              q_ref, ka_ref, vat_ref, kb_ref, vbt_ref):
    def tab(i):
        return tab_ref[:, i * LANES:(i + 1) * LANES]

    u = (_rms(h) * gattn_ref[...]).astype(BF16)
    proj = jnp.dot(u, w1_ref[...], preferred_element_type=F32)

    caq, saq = tab(T_CA) * QA_SCALE, tab(T_SA) * QA_SCALE
    for i in range(SWA_Q_W // LANES):
        sl = slice(C_QA + i * LANES, C_QA + (i + 1) * LANES)
        q_ref[r, :, i * LANES:(i + 1) * LANES] = _rope(proj[:, sl], caq, saq).astype(BF16)
    ka_ref[r] = _rope(proj[:, C_KA:C_KA + LANES], tab(T_CA), tab(T_SA)).astype(BF16)
    vat = proj[:, C_VA:C_VA + LANES].T
    vat_ref[r, 0] = vat[:, :HALF].astype(BF16)
    vat_ref[r, 1] = vat[:, HALF:].astype(BF16)

    qln = (_rms(proj[:, C_QL:C_QL + MLA_Q_RANK]) * gq_ref[...]).astype(BF16)
    qb = jnp.dot(qln, wq_ref[...], preferred_element_type=F32)
    cbq, sbq = tab(T_CB) * QB_SCALE, tab(T_SB) * QB_SCALE
    for hd in range(MLA_HEADS):
        sl = slice(hd * LANES, (hd + 1) * LANES)
        q_ref[r, :, SWA_Q_W + hd * LANES:SWA_Q_W + (hd + 1) * LANES] = _rope(
            qb[:, sl], cbq, sbq).astype(BF16)

    kvn = _rms(proj[:, C_KVL:C_KVL + MLA_KV_RANK]) * gkv_ref[...]
    kn = jnp.dot(kvn.astype(BF16), wkn_ref[...], preferred_element_type=F32)
    kr = _rope(proj[:, C_KR:C_KR + LANES], tab(T_CB), tab(T_SB))
    for hd in range(MLA_HEADS):
        sl = slice(hd * LANES, (hd + 1) * LANES)
        kb_ref[r, :, sl] = (kn[:, sl] + kr).astype(BF16)
    kvn_t = kvn.T.astype(BF16)
    vbt_ref[r, 0] = jnp.dot(wvt_ref[...], kvn_t, preferred_element_type=F32).astype(BF16)


def _run_pipelined(tasks, depth):
    pending = []
    for score_fn, finish_fn in tasks:
        pending.append((finish_fn, score_fn()))
        if len(pending) > depth:
            fin, s = pending.pop(0)
            fin(s)
    for fin, s in pending:
        fin(s)


def _make_attn_kernel(front):
    def attn_kernel(sinks_ref, q_ref, ka_ref, vat_ref, kb_ref, vbt_ref, ka0_ref, vat0_ref, kb0_ref,
                    vbt0_ref, o_ref, m_s, acc_s, s_buf):
        needed = (pl.program_id(1) > 0) | (pl.program_id(0) < ROWS)
        pl.when(needed)(
            functools.partial(attn_body, sinks_ref, q_ref, ka_ref, vat_ref, kb_ref, vbt_ref, ka0_ref,
                              vat0_ref, kb0_ref, vbt0_ref, o_ref, m_s, acc_s, s_buf))

        @pl.when(jnp.logical_not(needed))
        def _():
            o_ref[...] = jnp.zeros(o_ref.shape, o_ref.dtype)

    def attn_body(sinks_ref, q_ref, ka_ref, vat_ref, kb_ref, vbt_ref, ka0_ref, vat0_ref, kb0_ref,
                  vbt0_ref, o_ref, m_s, acc_s, s_buf):
        j = pl.program_id(1)

        def with_ones(v):
            return jnp.concatenate([v, jnp.ones((ONES_ROWS, v.shape[1]), v.dtype)], axis=0)

        prev_half = jnp.maximum(2 * j - 1, 0)
        k_prev = jnp.where(j == 1, ka0_ref[0, HALF:, :],
                           ka_ref[0, pl.ds(pl.multiple_of(prev_half * HALF, HALF), HALF), :])
        v_prev = jnp.where(j == 1, vat0_ref[0, 1], vat_ref[0, prev_half])
        kwin = jnp.concatenate(
            [k_prev, ka_ref[0, pl.ds(pl.multiple_of(j * BLK, BLK), BLK), :]], axis=0)
        lane = lax.broadcasted_iota(jnp.int32, (1, LANES), 1)
        first_kv = (lane % (LANES // 2)) < (SWA_HEAD_DIM // 2)
        zero = jnp.zeros_like(kwin)
        kg = (jnp.where(first_kv, kwin, zero), jnp.where(first_kv, zero, kwin))
        vwin = jnp.concatenate([v_prev, vat_ref[0, 2 * j], vat_ref[0, 2 * j + 1]], axis=1)
        vg_aug = [with_ones(vwin[g * SWA_HEAD_DIM:(g + 1) * SWA_HEAD_DIM, :])
                  for g in range(SWA_KV_HEADS)]
        rr = lax.broadcasted_iota(jnp.int32, (BLK, BLK), 0)
        cc = lax.broadcasted_iota(jnp.int32, (BLK, BLK), 1)
        d = (cc % HALF) + HALF - rr
        band = (d >= 0) & (d < WINDOW)
        swa_valid = [band & (rr + (j * BLK - HALF + hq * HALF) >= front) for hq in range(2)]
        first_head_cols = lax.broadcasted_iota(jnp.int32, (1, BLK), 1) < HALF

        def swa_task(pair, hq, g):
            hd0, hd1 = pair[0] + SWA_GROUP * g, pair[1] + SWA_GROUP * g
            keys = slice(hq * HALF, hq * HALF + BLK)
            cols = slice(hq * HALF, (hq + 1) * HALF)

            def score():
                w = jnp.concatenate([q_ref[0, cols, i * LANES:(i + 1) * LANES] for i in pair],
                                    axis=0)
                return _nt(kg[g][keys], w)

            def finish(s):
                s = jnp.where(swa_valid[hq], s, NEG)
                sink = jnp.where(first_head_cols, sinks_ref[hd0], sinks_ref[hd1]) * LOG2E
                m = jnp.maximum(jnp.max(s, axis=0, keepdims=True), sink)
                p = jnp.exp2(s - m).astype(BF16)
                o = jnp.dot(vg_aug[g][:, keys], p, preferred_element_type=F32)
                l = o[SWA_HEAD_DIM:SWA_HEAD_DIM + 1] + jnp.exp2(sink - m)
                on = (o[:SWA_HEAD_DIM] * (1.0 / l)).astype(o_ref.dtype)
                o_ref[0, 0, hd0 * SWA_HEAD_DIM:(hd0 + 1) * SWA_HEAD_DIM, cols] = on[:, :HALF]
                o_ref[0, 0, hd1 * SWA_HEAD_DIM:(hd1 + 1) * SWA_HEAD_DIM, cols] = on[:, HALF:]

            return score, finish

        def qh(hd):
            return q_ref[0, :, SWA_Q_W + hd * LANES:SWA_Q_W + (hd + 1) * LANES]

        def kblk(i, hd):
            return kb_ref[0, pl.ds(pl.multiple_of(i * BLK, BLK), BLK), hd * LANES:(hd + 1) * LANES]

        diag_mask = (rr <= cc) & (rr + j * BLK >= front)

        def v_rows(hd):
            return slice(hd * MLA_V_DIM, (hd + 1) * MLA_V_DIM)

        def acc_rows(hd):
            return slice(hd * ACC_ROWS, (hd + 1) * ACC_ROWS)

        def first_task(hd):
            def score():
                km = kb0_ref[0, front:BLK, hd * LANES:(hd + 1) * LANES]
                return _nt(kblk(j, hd), qh(hd)), _nt(km, qh(hd))

            def finish(ss):
                sd = jnp.where(diag_mask, ss[0], NEG)
                sm = jnp.where(j >= 1, ss[1], NEG)
                m = jnp.maximum(jnp.max(sd, axis=0, keepdims=True), jnp.max(sm, axis=0, keepdims=True))
                pd = jnp.exp2(sd - m).astype(BF16)
                pm = jnp.exp2(sm - m).astype(BF16)
                m_s[hd:hd + 1, :] = m
                vd = with_ones(vbt_ref[0, j, v_rows(hd), :])
                vm = with_ones(vbt0_ref[0, 0, v_rows(hd), front:BLK])
                acc_s[acc_rows(hd), :] = (jnp.dot(vd, pd, preferred_element_type=F32)
                                          + jnp.dot(vm, pm, preferred_element_type=F32))

            return score, finish

        def block_score(i, hd, slot):
            s_buf[slot] = _nt(kblk(i, hd), qh(hd))

        def block_finish(i, hd, slot):
            s = s_buf[slot]
            m_old = m_s[hd:hd + 1, :]
            m_new = jnp.maximum(m_old, jnp.max(s, axis=0, keepdims=True))
            alpha = jnp.exp2(m_old - m_new)
            p = jnp.exp2(s - m_new).astype(BF16)
            m_s[hd:hd + 1, :] = m_new
            vi = with_ones(vbt_ref[0, i, v_rows(hd), :])
            acc_s[acc_rows(hd), :] = acc_s[acc_rows(hd), :] * alpha + jnp.dot(
                vi, p, preferred_element_type=F32)

        def lookahead_task(hd):
            return (lambda: _nt(kb_ref[0, BLK:2 * BLK, hd * LANES:(hd + 1) * LANES], qh(hd)),
                    lambda s: s_buf.__setitem__(hd, s))

        tasks = [lookahead_task(hd) for hd in range(PIPE_DEPTH)]
        tasks += [swa_task(pair, hq, g) for pair in ((0, 1), (2, 3)) for hq in range(2)
                  for g in range(SWA_KV_HEADS)]
        tasks += [first_task(hd) for hd in range(MLA_HEADS)]
        _run_pipelined(tasks, PIPE_DEPTH)

        def run_blocks(first_blk, nb):
            for t in range(nb * MLA_HEADS):
                la = t + PIPE_DEPTH
                block_score(first_blk + la // MLA_HEADS, la % MLA_HEADS, la % S_SLOTS)
                block_finish(first_blk + t // MLA_HEADS, t % MLA_HEADS, t % S_SLOTS)

        n_blocks = jnp.maximum(j - 1, 0)
        n_triples = n_blocks // 3
        rest = n_blocks - 3 * n_triples
        rest_first = 1 + 3 * n_triples

        def triple_body(k, carry):
            run_blocks(1 + 3 * k, 3)
            return carry

        def pair_body(k, carry):
            run_blocks(rest_first, 2)
            return carry

        def single_body(k, carry):
            run_blocks(rest_first, 1)
            return carry

        lax.fori_loop(0, n_triples, triple_body, 0)
        lax.fori_loop(0, (rest == 2).astype(jnp.int32), pair_body, 0)
        lax.fori_loop(0, (rest == 1).astype(jnp.int32), single_body, 0)

        for hd in range(MLA_HEADS):
            acc = acc_s[acc_rows(hd), :]
            o = acc[:MLA_V_DIM] * (1.0 / acc[MLA_V_DIM:MLA_V_DIM + 1])
            o_ref[0, 0, SWA_Q_W + hd * MLA_V_DIM:SWA_Q_W + (hd + 1) * MLA_V_DIM, :] = o.astype(o_ref.dtype)

    return attn_kernel


def _post_body(r, h, o_ref, gmix_ref, wo_ref, gffn_ref, wg_ref, wu_ref, wd_ref):
    a = o_ref[r, 0].astype(F32)
    aa, bb = a[:SWA_Q_W], a[SWA_Q_W:]
    ia = lax.rsqrt(jnp.mean(aa * aa, axis=0, keepdims=True) + EPS)
    ib = lax.rsqrt(jnp.mean(bb * bb, axis=0, keepdims=True) + EPS)
    y = jnp.concatenate([aa * ia, bb * ib], axis=0)
    mix = (y.T * gmix_ref[...]).astype(BF16)
    h1 = h + jnp.dot(mix, wo_ref[...], preferred_element_type=F32)
    u = (_rms(h1) * gffn_ref[...]).astype(BF16)
    g = jnp.dot(u, wg_ref[...], preferred_element_type=F32)
    up = jnp.dot(u, wu_ref[...], preferred_element_type=F32)
    act = (g * jax.nn.sigmoid(g) * up).astype(BF16)
    return h1 + jnp.dot(act, wd_ref[...], preferred_element_type=F32)


N_POST_W = 6
N_PRE_W = 8


def _block_is_needed():
    return (pl.program_id(0) > 0) | (pl.program_id(1) == 0)


def _zero_skipped(out_refs):
    @pl.when(jnp.logical_not(_block_is_needed()))
    def _():
        for ref in out_refs:
            ref[...] = jnp.zeros(ref.shape, ref.dtype)


def _first_kernel(front, x_ref, meta_ref, *refs):
    pre_w, h_out, pre_out = refs[:N_PRE_W], refs[N_PRE_W], refs[N_PRE_W + 1:]

    @pl.when(_block_is_needed())
    def _():
        meta_blk = jnp.concatenate([jnp.zeros((front, meta_ref.shape[1]), F32), meta_ref[...]],
                                   axis=0)
        for r in range(ROWS):
            h = jnp.where(pl.program_id(0) == 0, meta_blk, x_ref[r])
            h_out[r] = h
            _pre_body(r, h, *pre_w, *pre_out)

    _zero_skipped((h_out,) + tuple(pre_out))


def _mid_kernel(h_ref, o_ref, *refs):
    post_w, pre_w = refs[:N_POST_W], refs[N_POST_W:N_POST_W + N_PRE_W]
    h_out, pre_out = refs[N_POST_W + N_PRE_W], refs[N_POST_W + N_PRE_W + 1:]

    @pl.when(_block_is_needed())
    def _():
        hs = [_post_body(r, h_ref[r], o_ref, *post_w) for r in range(ROWS)]
        for r in range(ROWS):
            h_out[r] = hs[r]
            _pre_body(r, hs[r], *pre_w, *pre_out)

    _zero_skipped((h_out,) + tuple(pre_out))


def _last_kernel(h_ref, o_ref, *refs):
    post_w, gfinal_ref, out_ref = refs[:N_POST_W], refs[N_POST_W], refs[N_POST_W + 1]
    for r in range(ROWS):
        h = _post_body(r, h_ref[r], o_ref, *post_w)
        out_ref[r] = _rms(h) * gfinal_ref[...]


def _gather_cols(w, idx):
    idx = list(idx)
    parts, i = [], 0
    while i < len(idx):
        k = i + 1
        if idx[i] < 0:
            while k < len(idx) and idx[k] < 0:
                k += 1
            parts.append(jnp.zeros(w.shape[:-1] + (k - i,), w.dtype))
        else:
            while k < len(idx) and idx[k] == idx[k - 1] + 1:
                k += 1
            parts.append(w[..., idx[i]:idx[i] + (k - i)])
        i = k
    return jnp.concatenate(parts, axis=-1)


def _w1_index(o1, o2, o3, o4, o5):
    half = SWA_HEAD_DIM // 2
    idx = []
    for i in range(SWA_GROUP):
        pair = (i, i + SWA_GROUP)
        for part in range(2):
            for hd in pair:
                idx += [hd * SWA_HEAD_DIM + part * half + t for t in range(half)]
    for part in range(2):
        for g in range(SWA_KV_HEADS):
            idx += [o1 + g * SWA_HEAD_DIM + part * half + t for t in range(half)]
    idx += list(range(o2, o5))
    idx += _mla_group_index(o5, None)
    return idx


def _mla_group_index(rope_base, nope_base):
    hn, hr = MLA_NOPE_DIM // 2, MLA_ROPE_DIM // 2
    pad = LANES // 2 - hn - hr
    idx = []
    for part in range(2):
        idx += [-1] * hn if nope_base is None else [nope_base + part * hn + t for t in range(hn)]
        idx += [-1] * hr if rope_base is None else [rope_base + part * hr + t for t in range(hr)]
        idx += [-1] * pad
    return idx


def _rope_tables(tp, front):
    pos = (jnp.arange(tp) - front).astype(F32)

    def cs(d):
        inv = ROPE_THETA ** (-jnp.arange(0, d, 2, dtype=F32) / d)
        ang = pos[:, None] * inv[None, :]
        return jnp.cos(ang), jnp.sin(ang)

    ca, sa = cs(SWA_HEAD_DIM)
    CA = jnp.concatenate([ca] * 4, axis=1)
    SA = jnp.concatenate([-sa, -sa, sa, sa], axis=1)
    cb, sb = cs(MLA_ROPE_DIM)
    hn, hr = MLA_NOPE_DIM // 2, MLA_ROPE_DIM // 2
    pad = LANES // 2 - hn - hr
    one_n, one_p = jnp.ones((tp, hn), F32), jnp.ones((tp, pad), F32)
    z_n, z_p = jnp.zeros((tp, hn), F32), jnp.zeros((tp, pad), F32)
    CB = jnp.concatenate([one_n, cb, one_p, one_n, cb, one_p], axis=1)
    SB = jnp.concatenate([z_n, -sb, z_p, z_n, sb, z_p], axis=1)
    return jnp.concatenate([CA, SA, CB, SB], axis=1)


def _const_spec(shape):
    return pl.BlockSpec(shape, lambda *_: (0,) * len(shape), pipeline_mode=pl.Buffered(1))


def _layer_spec(l, shape):
    return pl.BlockSpec((None,) + shape, lambda *_: (l,) + (0,) * len(shape),
                        pipeline_mode=pl.Buffered(1))


def kernel(x, meta_tokens, attn_norm, w_in, q_norm, w_q_up, kv_norm, w_kv_up, sinks,
           out_norm_swa, out_norm_mla, w_o, ffn_norm, w_gate, w_up, w_down, final_norm):
    B, S, D = x.shape
    depth = w_in.shape[0]
    d_ff = w_gate.shape[-1]
    assert S % BLK == 0 and N_META <= BLK and WINDOW <= HALF and MLA_HEADS % S_SLOTS == 0 and B % ROWS == 0
    front = BLK - N_META
    tp = BLK + S
    nblk = tp // BLK

    o1 = SWA_Q_W
    o2 = o1 + SWA_KV_W
    o3 = o2 + SWA_KV_W
    o4 = o3 + MLA_Q_RANK
    o5 = o4 + MLA_KV_RANK

    w1 = _gather_cols(w_in, _w1_index(o1, o2, o3, o4, o5)).astype(BF16)
    wq_idx, wkn_idx, wv_idx = [], [], []
    for hd in range(MLA_HEADS):
        wq_idx += _mla_group_index(hd * MLA_QK_DIM + MLA_NOPE_DIM, hd * MLA_QK_DIM)
        wkn_idx += _mla_group_index(None, hd * (MLA_NOPE_DIM + MLA_V_DIM))
        wv_idx += [hd * (MLA_NOPE_DIM + MLA_V_DIM) + MLA_NOPE_DIM + t for t in range(MLA_V_DIM)]
    wq = _gather_cols(w_q_up, wq_idx).astype(BF16)
    wkn = _gather_cols(w_kv_up, wkn_idx).astype(BF16)
    wvt = jnp.swapaxes(_gather_cols(w_kv_up, wv_idx), 1, 2).astype(BF16)
    wo = w_o.astype(BF16)
    wg = w_gate.astype(BF16)
    wu = w_up.astype(BF16)
    wd = w_down.astype(BF16)
    gmix = jnp.concatenate([out_norm_swa, out_norm_mla], axis=-1)
    tabs = _rope_tables(tp, front)

    hw = MLA_HEADS * LANES
    h_spec = pl.BlockSpec((ROWS, BLK, D), lambda j, b: (b, j, 0))
    o_spec = pl.BlockSpec((ROWS, 1, MIX_W, BLK), lambda j, b: (b, j, 0, 0))
    def pre_w_specs(l):
        return [
            pl.BlockSpec((BLK, N_TABS * LANES), lambda j, b: (j, 0)),
            _layer_spec(l, (1, D)),
            _layer_spec(l, (D, W1_COLS)),
            _layer_spec(l, (1, MLA_Q_RANK)),
            _layer_spec(l, (MLA_Q_RANK, hw)),
            _layer_spec(l, (1, MLA_KV_RANK)),
            _layer_spec(l, (MLA_KV_RANK, hw)),
            _layer_spec(l, (MLA_OUT_W, MLA_KV_RANK)),
        ]

    def post_w_specs(l):
        return [
            _layer_spec(l, (1, MIX_W)),
            _layer_spec(l, (MIX_W, D)),
            _layer_spec(l, (1, D)),
            _layer_spec(l, (D, d_ff)),
            _layer_spec(l, (D, d_ff)),
            _layer_spec(l, (d_ff, D)),
        ]
    pre_out_specs = [
        pl.BlockSpec((ROWS, BLK, SWA_Q_W + hw), lambda j, b: (b, j, 0)),
        pl.BlockSpec((ROWS, BLK, LANES), lambda j, b: (b, j, 0)),
        pl.BlockSpec((ROWS, 2, LANES, HALF), lambda j, b: (b, j, 0, 0)),
        pl.BlockSpec((ROWS, BLK, hw), lambda j, b: (b, j, 0)),
        pl.BlockSpec((ROWS, 1, MLA_OUT_W, BLK), lambda j, b: (b, j, 0, 0)),
    ]
    pre_out_shapes = [
        jax.ShapeDtypeStruct((B, tp, SWA_Q_W + hw), BF16),
        jax.ShapeDtypeStruct((B, tp, LANES), BF16),
        jax.ShapeDtypeStruct((B, 2 * nblk, LANES, HALF), BF16),
        jax.ShapeDtypeStruct((B, tp, hw), BF16),
        jax.ShapeDtypeStruct((B, nblk, MLA_OUT_W, BLK), BF16),
    ]
    h_shape = jax.ShapeDtypeStruct((B, tp, D), F32)
    seq_params = dict(dimension_semantics=("arbitrary", "arbitrary"))

    pre_w = (tabs, attn_norm[:, None], w1, q_norm[:, None], wq, kv_norm[:, None], wkn, wvt)
    post_w = (gmix[:, None], wo, ffn_norm[:, None], wg, wu, wd)

    first_call = pl.pallas_call(
        functools.partial(_first_kernel, front),
        grid=(nblk, B // ROWS),
        in_specs=[pl.BlockSpec((ROWS, BLK, D), lambda j, b: (b, jnp.maximum(j - 1, 0), 0)),
                  _const_spec((N_META, D))] + pre_w_specs(0),
        out_specs=[h_spec] + pre_out_specs,
        out_shape=[h_shape] + pre_out_shapes,
        compiler_params=pltpu.CompilerParams(vmem_limit_bytes=40 * MIB, **seq_params),
        name="first",
    )

    attn_call = pl.pallas_call(
        _make_attn_kernel(front),
        grid=(B, nblk),
        in_specs=[
            pl.BlockSpec(memory_space=pltpu.SMEM),
            pl.BlockSpec((1, BLK, SWA_Q_W + hw), lambda b, j: (b, j, 0)),
            pl.BlockSpec((1, tp, LANES), lambda b, j: (b, 0, 0)),
            pl.BlockSpec((1, 2 * nblk, LANES, HALF), lambda b, j: (b, 0, 0, 0)),
            pl.BlockSpec((1, tp, hw), lambda b, j: (b, 0, 0)),
            pl.BlockSpec((1, nblk, MLA_OUT_W, BLK), lambda b, j: (b, 0, 0, 0)),
            _const_spec((1, BLK, LANES)),
            _const_spec((1, 2, LANES, HALF)),
            _const_spec((1, BLK, hw)),
            _const_spec((1, 1, MLA_OUT_W, BLK)),
        ],
        out_specs=pl.BlockSpec((1, 1, MIX_W, BLK), lambda b, j: (b, j, 0, 0)),
        out_shape=jax.ShapeDtypeStruct((B, nblk, MIX_W, BLK), BF16),
        scratch_shapes=[
            pltpu.VMEM((MLA_HEADS, BLK), F32),
            pltpu.VMEM((MLA_HEADS * ACC_ROWS, BLK), F32),
            pltpu.VMEM((S_SLOTS, BLK, BLK), F32),
        ],
        compiler_params=pltpu.CompilerParams(vmem_limit_bytes=48 * MIB, **seq_params),
        name="attn",
    )

    def mid_call(l):
        return pl.pallas_call(
            _mid_kernel,
            grid=(nblk, B // ROWS),
            in_specs=[h_spec, o_spec] + post_w_specs(l) + pre_w_specs(l + 1),
            out_specs=[h_spec] + pre_out_specs,
            out_shape=[h_shape] + pre_out_shapes,
            compiler_params=pltpu.CompilerParams(vmem_limit_bytes=60 * MIB, **seq_params),
            name="mid",
        )

    last_call = pl.pallas_call(
        _last_kernel,
        grid=(B // ROWS, S // BLK),
        in_specs=[pl.BlockSpec((ROWS, BLK, D), lambda b, j: (b, j + 1, 0)),
                  pl.BlockSpec((ROWS, 1, MIX_W, BLK), lambda b, j: (b, j + 1, 0, 0))]
                 + post_w_specs(depth - 1) + [_const_spec((1, D))],
        out_specs=pl.BlockSpec((ROWS, BLK, D), lambda b, j: (b, j, 0)),
        out_shape=jax.ShapeDtypeStruct((B, S, D), x.dtype),
        compiler_params=pltpu.CompilerParams(vmem_limit_bytes=56 * MIB, **seq_params),
        name="last",
    )

    h, *pre_out = first_call(x, meta_tokens.astype(F32), *pre_w)
    for l in range(depth):
        q, ka, vat, kb, vbt = pre_out
        o_t = attn_call(sinks[l], q, ka, vat, kb, vbt, ka, vat, kb, vbt)
        if l + 1 < depth:
            h, *pre_out = mid_call(l)(h, o_t, *post_w, *pre_w)
    return last_call(h, o_t, *post_w, final_norm[None])
```
